```python
import jax, jax.numpy as jnp
from jax import lax
import numpy as np

D_MODEL = 1024
BATCH = 2
SEQ = 8192
DEPTH = 1
DEC_BATCH = 32
DEC_SEQ = 64
PAST_LEN = 4096

CHUNK = 64
Q_BLOCK = 128
GLA_HEADS = 4
GLA_DK = D_MODEL // 2 // GLA_HEADS
GLA_DV = D_MODEL // GLA_HEADS
GLA_QK = GLA_HEADS * GLA_DK
GLA_V = GLA_HEADS * GLA_DV
GLA_GATE_RANK = 16
GLA_GATE_TEMP = 16.0
SB_HEADS = 8
SB_DH = D_MODEL // SB_HEADS
SB_W = SB_HEADS * SB_DH
D_FF = 2816
LN_EPS = 1e-5
DN_ALPHA = (2 * DEPTH) ** 0.25
DN_BETA = (8 * DEPTH) ** -0.25
SPLITS = (GLA_QK, GLA_QK, GLA_V, GLA_V, GLA_GATE_RANK, SB_W, SB_W, SB_W, D_MODEL, D_MODEL)
D_IN = sum(SPLITS)
SPLIT_POINTS = tuple(int(v) for v in np.cumsum(SPLITS)[:-1])

kernel_name = 'gla_stickbreaking_macaron_deepnorm_stream'


def layer_norm(x, g, b):
    xf = x.astype(jnp.float32)
    mu = jnp.mean(xf, axis=-1, keepdims=True)
    var = jnp.mean(jnp.square(xf - mu), axis=-1, keepdims=True)
    return ((xf - mu) * lax.rsqrt(var + LN_EPS) * g + b).astype(x.dtype)


def swiglu(x, w_in, w_out):
    gate, up = jnp.split(x @ w_in, 2, axis=-1)
    return (jax.nn.silu(gate) * up) @ w_out


def head_rms_norm(o, g):
    of = o.astype(jnp.float32)
    return of * lax.rsqrt(jnp.mean(jnp.square(of), axis=-1, keepdims=True) + LN_EPS) * g


def gla_chunked(q, k, v, log_a, s0):
    B, T, H, dk = q.shape
    dv = v.shape[-1]
    c = min(CHUNK, T)
    n = T // c

    def chunks(t):
        return jnp.moveaxis(t.reshape(B, n, c, H, t.shape[-1]), 1, 0)

    causal = jnp.tril(jnp.ones((c, c), dtype=bool))

    def step(S, inp):
        qc, kc, vc, gc = inp
        b = jnp.cumsum(gc, axis=1)
        b_last = b[:, -1]
        qg = qc * jnp.exp(b)
        kg = kc * jnp.exp(-b)
        kd = kc * jnp.exp(b_last[:, None] - b)
        att = jnp.where(causal, jnp.einsum('bthk,bshk->bhts', qg, kg), 0.0)
        o = jnp.einsum('bhts,bshv->bthv', att, vc) + jnp.einsum('bthk,bhkv->bthv', qg, S)
        S = jnp.exp(b_last)[..., None] * S + jnp.einsum('bshk,bshv->bhkv', kd, vc)
        return S, o

    S, o = lax.scan(step, s0.astype(jnp.float32), (chunks(q), chunks(k), chunks(v), chunks(log_a)))
    o = jnp.moveaxis(o, 0, 1).reshape(B, T, H, dv)
    return o, S.astype(s0.dtype)


def sb_block(qb, q_pos, k, v, k_pos):
    z = jnp.einsum('bqhd,bkhd->bhqk', qb, k).astype(jnp.float32) * (SB_DH ** -0.5)
    visible = k_pos[None, :] < q_pos[:, None]
    log_beta = jax.nn.log_sigmoid(z)
    log_keep = jnp.where(visible, jax.nn.log_sigmoid(-z), 0.0)
    rev = lax.cumsum(log_keep, axis=3, reverse=True)
    between = jnp.concatenate([rev[..., 1:], jnp.zeros_like(rev[..., :1])], axis=-1)
    w = jnp.exp(jnp.where(visible, log_beta + between, -jnp.inf))
    return jnp.einsum('bhqk,bkhd->bqhd', w, v).astype(qb.dtype)


def stick_breaking(q, k, v, n_past):
    B, T, H, d = q.shape
    blk = min(Q_BLOCK, T)
    nb = T // blk
    k_pos = jnp.arange(k.shape[1])
    q_pos = (n_past + jnp.arange(T)).reshape(nb, blk)
    q_blocks = jnp.moveaxis(q.reshape(B, nb, blk, H, d), 1, 0)
    out = lax.map(lambda a: sb_block(a[0], a[1], k, v, k_pos), (q_blocks, q_pos))
    return jnp.moveaxis(out, 0, 1).reshape(B, T, H, d)


def token_mix(h, s0, k_past, v_past, w_in, w_gla_gate_up, b_gla_gate, g_gla_norm, w_gla_o, w_sb_o, w_out):
    B, T, _ = h.shape
    gq, gk, gv, gr, glr, sq, sk, sv, ga, gb = jnp.split(h @ w_in, SPLIT_POINTS, axis=-1)
    q = gq.reshape(B, T, GLA_HEADS, GLA_DK) * (GLA_DK ** -0.5)
    k = gk.reshape(B, T, GLA_HEADS, GLA_DK)
    v = gv.reshape(B, T, GLA_HEADS, GLA_DV)
    log_a = (jax.nn.log_sigmoid((glr @ w_gla_gate_up + b_gla_gate).astype(jnp.float32))
             / GLA_GATE_TEMP).reshape(B, T, GLA_HEADS, GLA_DK)
    o_a, s_new = gla_chunked(q, k, v, log_a, s0)
    o_a = head_rms_norm(o_a, g_gla_norm).reshape(B, T, GLA_V).astype(h.dtype) * jax.nn.silu(gr)
    branch_a = o_a @ w_gla_o
    sq = sq.reshape(B, T, SB_HEADS, SB_DH)
    sk = sk.reshape(B, T, SB_HEADS, SB_DH)
    sv = sv.reshape(B, T, SB_HEADS, SB_DH)
    k_all = jnp.concatenate([k_past, sk], axis=1)
    v_all = jnp.concatenate([v_past, sv], axis=1)
    o_b = stick_breaking(sq, k_all, v_all, k_past.shape[1])
    branch_b = o_b.reshape(B, T, SB_W) @ w_sb_o
    merged = jax.nn.sigmoid(ga) * branch_a + jax.nn.sigmoid(gb) * branch_b
    return merged @ w_out, s_new, sk, sv


def encoder_layer(x, s0, k_past, v_past, ffn1_w_in, ffn1_w_out, ln1_g, ln1_b,
                  w_in, w_gla_gate_up, b_gla_gate, g_gla_norm, w_gla_o, w_sb_o, w_out, ln2_g, ln2_b,
                  ffn2_w_in, ffn2_w_out, ln3_g, ln3_b):
    h = layer_norm(DN_ALPHA * x + 0.5 * swiglu(x, ffn1_w_in, ffn1_w_out), ln1_g, ln1_b)
    mix, s_new, k_new, v_new = token_mix(h, s0, k_past, v_past, w_in, w_gla_gate_up, b_gla_gate,
                                         g_gla_norm, w_gla_o, w_sb_o, w_out)
    h = layer_norm(DN_ALPHA * h + mix, ln2_g, ln2_b)
    y = layer_norm(DN_ALPHA * h + 0.5 * swiglu(h, ffn2_w_in, ffn2_w_out), ln3_g, ln3_b)
    return y, s_new, k_new, v_new


def setup_inputs(seed: int = 0) -> dict:
    key = jax.random.key(seed)
    ks = iter(jax.random.split(key, 32))

    def nrm(shape, scale):
        return jax.random.normal(next(ks), shape, jnp.float32) * scale

    L = DEPTH
    return {
        'x_prompt': nrm((BATCH, SEQ, D_MODEL), 1.0),
        'x_sample': nrm((DEC_BATCH, DEC_SEQ, D_MODEL), 1.0),
        'state_gla': nrm((L, DEC_BATCH, GLA_HEADS, GLA_DK, GLA_DV), 1.0),
        'cache_sb_k': nrm((L, DEC_BATCH, PAST_LEN, SB_HEADS, SB_DH), 1.0),
        'cache_sb_v': nrm((L, DEC_BATCH, PAST_LEN, SB_HEADS, SB_DH), 1.0),
        'ffn1_w_in': nrm((L, D_MODEL, 2 * D_FF), D_MODEL ** -0.5),
        'ffn1_w_out': nrm((L, D_FF, D_MODEL), D_FF ** -0.5 * DN_BETA),
        'ln1_g': 1.0 + nrm((L, D_MODEL), 0.02),
        'ln1_b': nrm((L, D_MODEL), 0.02),
        'w_in': nrm((L, D_MODEL, D_IN), D_MODEL ** -0.5),
        'w_gla_gate_up': nrm((L, GLA_GATE_RANK, GLA_QK), GLA_GATE_RANK ** -0.5),
        'b_gla_gate': nrm((L, GLA_QK), 0.1),
        'g_gla_norm': 1.0 + nrm((L, GLA_HEADS, GLA_DV), 0.02),
        'w_gla_o': nrm((L, GLA_V, D_MODEL), GLA_V ** -0.5),
        'w_sb_o': nrm((L, SB_W, D_MODEL), SB_W ** -0.5),
        'w_out': nrm((L, D_MODEL, D_MODEL), D_MODEL ** -0.5 * DN_BETA),
        'ln2_g': 1.0 + nrm((L, D_MODEL), 0.02),
        'ln2_b': nrm((L, D_MODEL), 0.02),
        'ffn2_w_in': nrm((L, D_MODEL, 2 * D_FF), D_MODEL ** -0.5),
        'ffn2_w_out': nrm((L, D_FF, D_MODEL), D_FF ** -0.5 * DN_BETA),
        'ln3_g': 1.0 + nrm((L, D_MODEL), 0.02),
        'ln3_b': nrm((L, D_MODEL), 0.02),
    }


def reference(x_prompt, x_sample, state_gla, cache_sb_k, cache_sb_v,
              ffn1_w_in, ffn1_w_out, ln1_g, ln1_b,
              w_in, w_gla_gate_up, b_gla_gate, g_gla_norm, w_gla_o, w_sb_o, w_out, ln2_g, ln2_b,
              ffn2_w_in, ffn2_w_out, ln3_g, ln3_b):
    bp = x_prompt.shape[0]
    xp, xs = x_prompt, x_sample
    gla_p, k_p, v_p, gla_s, k_s, v_s = [], [], [], [], [], []
    for l in range(DEPTH):
        lw = (ffn1_w_in[l], ffn1_w_out[l], ln1_g[l], ln1_b[l],
              w_in[l], w_gla_gate_up[l], b_gla_gate[l], g_gla_norm[l], w_gla_o[l], w_sb_o[l], w_out[l],
              ln2_g[l], ln2_b[l], ffn2_w_in[l], ffn2_w_out[l], ln3_g[l], ln3_b[l])
        s0_p = jnp.zeros((bp, GLA_HEADS, GLA_DK, GLA_DV), xp.dtype)
        past_p = jnp.zeros((bp, 0, SB_HEADS, SB_DH), xp.dtype)
        xp, sp, kp, vp = encoder_layer(xp, s0_p, past_p, past_p, *lw)
        xs, ss, ksn, vsn = encoder_layer(xs, state_gla[l], cache_sb_k[l], cache_sb_v[l], *lw)
        gla_p.append(sp); k_p.append(kp); v_p.append(vp)
        gla_s.append(ss); k_s.append(ksn); v_s.append(vsn)
    return (xp, xs, jnp.stack(gla_p), jnp.stack(k_p), jnp.stack(v_p),
            jnp.stack(gla_s), jnp.stack(k_s), jnp.stack(v_s))
```

```python
import functools

import jax
import jax.numpy as jnp
from jax import lax
from jax.experimental import pallas as pl
from jax.experimental.pallas import tpu as pltpu

F32 = jnp.float32
BF16 = jnp.bfloat16

D_MODEL = 1024
CHUNK = 64
GLA_HEADS = 4
GLA_DK = 128
GLA_DV = 256
GLA_QK = GLA_HEADS * GLA_DK
GLA_V = GLA_HEADS * GLA_DV
GLA_GATE_RANK = 16
GLA_GATE_TEMP = 16.0
SB_HEADS = 8
SB_DH = 128
SB_W = SB_HEADS * SB_DH
D_FF = 2816
LN_EPS = 1e-5
DEPTH = 1
DN_ALPHA = (2 * DEPTH) ** 0.25

LANES = 128
FF_CHUNK = 256
ROW_TILE = 512
PROJ_ROW_TILE = 256
SB_BLOCK = 128
VMEM_LIMIT = 56 * 1024 * 1024


def _layer_norm(r, g, b):
    mu = jnp.mean(r, axis=-1, keepdims=True)
    d = r - mu
    var = jnp.mean(d * d, axis=-1, keepdims=True)
    return d * lax.rsqrt(var + LN_EPS) * g + b


def _softplus_neg_abs(z):
    return jnp.log(1.0 + jnp.exp(-jnp.abs(z)))


def _split3(x):
    x1 = x.astype(BF16)
    r1 = x - x1.astype(F32)
    x2 = r1.astype(BF16)
    r2 = r1 - x2.astype(F32)
    return x1, x2, r2.astype(BF16)


def _dot(a, b):
    return jnp.dot(a, b, preferred_element_type=F32)


def _dot_nt(a, b):
    return lax.dot_general(a, b, (((1,), (1,)), ((), ())), preferred_element_type=F32)


def _dot_tn(a, b):
    return lax.dot_general(a, b, (((0,), (0,)), ((), ())), preferred_element_type=F32)


def _const_spec(shape):
    nd = len(shape)
    return pl.BlockSpec(shape, lambda *_: (0,) * nd, pipeline_mode=pl.Buffered(1))


def _params(*sem):
    return pltpu.CompilerParams(dimension_semantics=sem, vmem_limit_bytes=VMEM_LIMIT)


def _ffn_ln_kernel(x_ref, wg_ref, wu_ref, wo_ref, g_ref, b_ref, o_ref, acc_ref):
    x = x_ref[...]
    xb = x.astype(BF16)
    acc_ref[...] = jnp.zeros_like(acc_ref)

    def step(j, carry):
        gate = _dot(xb, wg_ref[j])
        up = _dot(xb, wu_ref[j])
        a = (gate * jax.nn.sigmoid(gate) * up).astype(BF16)
        acc_ref[...] += _dot(a, wo_ref[j])
        return carry

    lax.fori_loop(0, D_FF // FF_CHUNK, step, 0)
    r = DN_ALPHA * x + 0.5 * acc_ref[...]
    o_ref[...] = _layer_norm(r, g_ref[...], b_ref[...])


def _ffn_ln(x, wg, wu, wo, g, b):
    m = x.shape[0]
    tm = min(ROW_TILE, m)
    row = pl.BlockSpec((tm, D_MODEL), lambda i: (i, 0))
    return pl.pallas_call(
        _ffn_ln_kernel,
        grid=(m // tm,),
        in_specs=[row, _const_spec(wg.shape), _const_spec(wu.shape), _const_spec(wo.shape),
                  _const_spec(g.shape), _const_spec(b.shape)],
        out_specs=row,
        out_shape=jax.ShapeDtypeStruct((m, D_MODEL), F32),
        scratch_shapes=[pltpu.VMEM((tm, D_MODEL), F32)],
        compiler_params=_params("parallel"),
        name="ffn_ln",
    )(x, wg, wu, wo, g, b)


_PROJ_GROUPS = (("gq", GLA_QK), ("gk", GLA_QK), ("gv", GLA_V), ("gr", GLA_V),
                ("sq", SB_W), ("sk", SB_W), ("sv", SB_W), ("ga", D_MODEL), ("gb", D_MODEL))


def _proj_kernel(h_ref, w_ref, wlr_ref, wup_ref, bup_ref,
                 gq_ref, gk_ref, gv_ref, gr_ref, la_ref, sq_ref, sk_ref, sv_ref, ga_ref, gb_ref,
                 skb_ref, svb_ref):
    hb = h_ref[...].astype(BF16)
    outs = dict(gq=gq_ref, gk=gk_ref, gv=gv_ref, gr=gr_ref, sq=sq_ref, sk=sk_ref, sv=sv_ref,
                ga=ga_ref, gb=gb_ref)
    lo = 0
    for name, width in _PROJ_GROUPS:
        y = _dot(hb, w_ref[:, lo:lo + width])
        lo += width
        if name == "gq":
            y = y * (GLA_DK ** -0.5)
        if name == "sq":
            outs[name][...] = y.astype(BF16)
            continue
        outs[name][...] = y
        if name == "sk":
            skb_ref[...] = y.astype(BF16)
        if name == "sv":
            svb_ref[...] = y.astype(BF16)
    lr = _dot(hb, wlr_ref[...]).astype(BF16)
    pre = _dot(lr, wup_ref[...]) + bup_ref[...]
    la_ref[...] = (jnp.minimum(pre, 0.0) - _softplus_neg_abs(pre)) * (1.0 / GLA_GATE_TEMP)


def _proj(h, w_main, w_lr, w_up, b_up):
    m = h.shape[0]
    tm = min(PROJ_ROW_TILE, m)

    def row(width):
        return pl.BlockSpec((tm, width), lambda i: (i, 0))

    def out(width, dtype=F32):
        return jax.ShapeDtypeStruct((m, width), dtype)

    widths = (GLA_QK, GLA_QK, GLA_V, GLA_V, GLA_QK, SB_W, SB_W, SB_W, D_MODEL, D_MODEL, SB_W, SB_W)
    dtypes = (F32, F32, F32, F32, F32, BF16, F32, F32, F32, F32, BF16, BF16)
    return pl.pallas_call(
        _proj_kernel,
        grid=(m // tm,),
        in_specs=[row(D_MODEL), _const_spec(w_main.shape), _const_spec(w_lr.shape),
                  _const_spec(w_up.shape), _const_spec(b_up.shape)],
        out_specs=[row(w) for w in widths],
        out_shape=[out(w, d) for w, d in zip(widths, dtypes)],
        compiler_params=_params("parallel"),
        name="proj",
    )(h, w_main, w_lr, w_up, b_up)


def _gla_kernel(q_ref, k_ref, v_ref, la_ref, s0_ref, o_ref, sout_ref, s_scr, *, n_chunks):
    t = pl.program_id(1)

    @pl.when(t == 0)
    def _():
        s_scr[...] = s0_ref[0]

    rows = lax.broadcasted_iota(jnp.int32, (CHUNK, CHUNK), 0)
    cols = lax.broadcasted_iota(jnp.int32, (CHUNK, CHUNK), 1)
    causal = cols <= rows
    tri = causal.astype(BF16)

    def chunk(c, carry):
        r0 = pl.multiple_of(c * CHUNK, CHUNK)
        rs = pl.ds(r0, CHUNK)
        for h in range(GLA_HEADS):
            ks = slice(h * GLA_DK, (h + 1) * GLA_DK)
            vs = slice(h * GLA_DV, (h + 1) * GLA_DV)
            g = la_ref[0, rs, ks]
            g1, g2, g3 = _split3(g)
            b = _dot(tri, g1) + _dot(tri, g2) + _dot(tri, g3)
            b_last = b[CHUNK - 1:CHUNK, :]
            q = q_ref[0, rs, ks]
            k = k_ref[0, rs, ks]
            v = v_ref[0, rs, vs].astype(BF16)
            qg = (q * jnp.exp(b)).astype(BF16)
            kg = (k * jnp.exp(-b)).astype(BF16)
            kd = (k * jnp.exp(b_last - b)).astype(BF16)
            att = jnp.where(causal, _dot_nt(qg, kg), 0.0).astype(BF16)
            s_old = s_scr[h]
            o_ref[0, rs, vs] = _dot(att, v) + _dot(qg, s_old.astype(BF16))
            decay = jnp.exp(jnp.broadcast_to(b_last, (GLA_DK, GLA_DK))).T
            decay = jnp.concatenate([decay, decay], axis=1)
            s_scr[h] = decay * s_old + _dot_tn(kd, v)
        return carry

    lax.fori_loop(0, n_chunks, chunk, 0)

    @pl.when(t == pl.num_programs(1) - 1)
    def _():
        sout_ref[0] = s_scr[...]


def _gla(q, k, v, la, s0):
    bsz, t, _ = q.shape
    tb = min(512, t)
    qk_spec = pl.BlockSpec((1, tb, GLA_QK), lambda b, i: (b, i, 0))
    v_spec = pl.BlockSpec((1, tb, GLA_V), lambda b, i: (b, i, 0))
    s_spec = pl.BlockSpec((1, GLA_HEADS, GLA_DK, GLA_DV), lambda b, i: (b, 0, 0, 0))
    return pl.pallas_call(
        functools.partial(_gla_kernel, n_chunks=tb // CHUNK),
        grid=(bsz, t // tb),
        in_specs=[qk_spec, qk_spec, v_spec, qk_spec, s_spec],
        out_specs=[v_spec, s_spec],
        out_shape=[jax.ShapeDtypeStruct((bsz, t, GLA_V), F32),
                   jax.ShapeDtypeStruct((bsz, GLA_HEADS, GLA_DK, GLA_DV), F32)],
        scratch_shapes=[pltpu.VMEM((GLA_HEADS, GLA_DK, GLA_DV), F32)],
        compiler_params=_params("parallel", "arbitrary"),
        name="gla",
    )(q, k, v, la, s0)


def _suffix_matrix(tk):
    r = lax.broadcasted_iota(jnp.int32, (tk, tk + LANES), 0)
    c = lax.broadcasted_iota(jnp.int32, (tk, tk + LANES), 1)
    return ((r >= c) | (c >= tk)).astype(BF16)


def _sb_update(q, k, v, carry, acc, suffix, visible):
    tk = k.shape[0]
    z = _dot_nt(q, k) * (SB_DH ** -0.5)
    sp = _softplus_neg_abs(z)
    log_beta = jnp.minimum(z, 0.0) - sp
    log_keep = -jnp.maximum(z, 0.0) - sp
    if visible is not None:
        log_keep = jnp.where(visible, log_keep, 0.0)
    l1, l2, l3 = _split3(log_keep)
    sums = _dot(l1, suffix) + _dot(l2, suffix) + _dot(l3, suffix)
    incl = sums[:, :tk]
    total = sums[:, tk:]
    reps = max(tk // LANES, 1)
    carry_b = carry[:, :tk] if tk < LANES else jnp.concatenate([carry] * reps, axis=1)
    e = log_beta + (incl - log_keep) + carry_b
    if visible is not None:
        e = jnp.where(visible, e, -jnp.inf)
    w = jnp.exp(e).astype(BF16)
    return carry + total, acc + _dot(w, v)


def _sb_prompt_kernel(q_ref, k_ref, v_ref, o_ref):
    i = pl.program_id(2)
    blk = SB_BLOCK
    q = q_ref[0]
    suffix = _suffix_matrix(blk)
    rows = lax.broadcasted_iota(jnp.int32, (blk, blk), 0)
    cols = lax.broadcasted_iota(jnp.int32, (blk, blk), 1)
    r0 = pl.multiple_of(i * blk, blk)
    carry = jnp.zeros((blk, LANES), F32)
    acc = jnp.zeros((blk, SB_DH), F32)
    carry, acc = _sb_update(q, k_ref[0, pl.ds(r0, blk), :], v_ref[0, pl.ds(r0, blk), :],
                            carry, acc, suffix, cols < rows)

    def body(step, state):
        c, a = state
        k0 = pl.multiple_of((i - 1 - step) * blk, blk)
        return _sb_update(q, k_ref[0, pl.ds(k0, blk), :], v_ref[0, pl.ds(k0, blk), :], c, a, suffix, None)

    carry, acc = lax.fori_loop(0, i, body, (carry, acc))
    o_ref[0] = acc


def _sb_prompt(sq, sk, sv):
    bsz, t, _ = sq.shape
    q_spec = pl.BlockSpec((1, SB_BLOCK, SB_DH), lambda b, h, i: (b, i, h))
    kv_spec = pl.BlockSpec((1, t, SB_DH), lambda b, h, i: (b, 0, h))
    return pl.pallas_call(
        _sb_prompt_kernel,
        grid=(bsz, SB_HEADS, t // SB_BLOCK),
        in_specs=[q_spec, kv_spec, kv_spec],
        out_specs=q_spec,
        out_shape=jax.ShapeDtypeStruct((bsz, t, SB_W), F32),
        compiler_params=_params("parallel", "parallel", "arbitrary"),
        name="sb_prompt",
    )(sq, sk, sv)


def _sb_sample_kernel(q_ref, kn_ref, vn_ref, kp_ref, vp_ref, o_ref, *, n_past_blocks):
    tq = q_ref.shape[1]
    q = q_ref[0]
    rows = lax.broadcasted_iota(jnp.int32, (tq, tq), 0)
    cols = lax.broadcasted_iota(jnp.int32, (tq, tq), 1)
    carry = jnp.zeros((tq, LANES), F32)
    acc = jnp.zeros((tq, SB_DH), F32)
    carry, acc = _sb_update(q, kn_ref[0], vn_ref[0], carry, acc, _suffix_matrix(tq), cols < rows)
    suffix = _suffix_matrix(SB_BLOCK)

    def body(step, state):
        c, a = state
        k0 = pl.multiple_of((n_past_blocks - 1 - step) * SB_BLOCK, SB_BLOCK)
        k = kp_ref[0, pl.ds(k0, SB_BLOCK), :].astype(BF16)
        v = vp_ref[0, pl.ds(k0, SB_BLOCK), :].astype(BF16)
        return _sb_update(q, k, v, c, a, suffix, None)

    carry, acc = lax.fori_loop(0, n_past_blocks, body, (carry, acc))
    o_ref[0] = acc


def _sb_sample(sq, sk, sv, k_past, v_past):
    bsz, t, _ = sq.shape
    p = k_past.shape[1]
    new_spec = pl.BlockSpec((1, t, SB_DH), lambda b, h: (b, 0, h))
    past_spec = pl.BlockSpec((1, p, SB_DH), lambda b, h: (b, 0, h))
    return pl.pallas_call(
        functools.partial(_sb_sample_kernel, n_past_blocks=p // SB_BLOCK),
        grid=(bsz, SB_HEADS),
        in_specs=[new_spec, new_spec, new_spec, past_spec, past_spec],
        out_specs=new_spec,
        out_shape=jax.ShapeDtypeStruct((bsz, t, SB_W), F32),
        compiler_params=_params("parallel", "parallel"),
        name="sb_sample",
    )(sq, sk, sv, k_past, v_past)


def _merge_ln_kernel(h_ref, oa_ref, gr_ref, ob_ref, ga_ref, gb_ref, gn_ref,
                     wa_ref, wb_ref, wo_ref, g_ref, b_ref, o_ref):
    o = oa_ref[...]
    gn = gn_ref[...]
    parts = []
    for hd in range(GLA_HEADS):
        vs = slice(hd * GLA_DV, (hd + 1) * GLA_DV)
        oh = o[:, vs]
        ms = jnp.mean(oh * oh, axis=-1, keepdims=True)
        parts.append(oh * lax.rsqrt(ms + LN_EPS) * gn[:, vs])
    gr = gr_ref[...]
    oa = jnp.concatenate(parts, axis=1) * (gr * jax.nn.sigmoid(gr))
    branch_a = _dot(oa.astype(BF16), wa_ref[...])
    branch_b = _dot(ob_ref[...].astype(BF16), wb_ref[...])
    merged = jax.nn.sigmoid(ga_ref[...]) * branch_a + jax.nn.sigmoid(gb_ref[...]) * branch_b
    mix = _dot(merged.astype(BF16), wo_ref[...])
    o_ref[...] = _layer_norm(DN_ALPHA * h_ref[...] + mix, g_ref[...], b_ref[...])


def _merge_ln(h, oa, gr, ob, ga, gb, gn, wa, wb, wo, g, b):
    m = h.shape[0]
    tm = min(ROW_TILE, m)
    row = pl.BlockSpec((tm, D_MODEL), lambda i: (i, 0))
    consts = [gn, wa, wb, wo, g, b]
    return pl.pallas_call(
        _merge_ln_kernel,
        grid=(m // tm,),
        in_specs=[row] * 6 + [_const_spec(c.shape) for c in consts],
        out_specs=row,
        out_shape=jax.ShapeDtypeStruct((m, D_MODEL), F32),
        compiler_params=_params("parallel"),
        name="merge_ln",
    )(h, oa, gr, ob, ga, gb, *consts)


def _prep_ffn(w_in, w_out):
    n = D_FF // FF_CHUNK
    wg = w_in[:, :D_FF].reshape(D_MODEL, n, FF_CHUNK).transpose(1, 0, 2).astype(BF16)
    wu = w_in[:, D_FF:].reshape(D_MODEL, n, FF_CHUNK).transpose(1, 0, 2).astype(BF16)
    wo = w_out.reshape(n, FF_CHUNK, D_MODEL).astype(BF16)
    return wg, wu, wo


def _row(v):
    return v.reshape(1, -1)


def _layer(x, s0, k_past, v_past, wts):
    bsz, t, _ = x.shape
    m = bsz * t
    h = _ffn_ln(x.reshape(m, D_MODEL), *wts["ffn1"])
    gq, gk, gv, gr, la, sq, sk, sv, ga, gb, skb, svb = _proj(h, *wts["proj"])

    def seq(a):
        return a.reshape(bsz, t, a.shape[-1])

    o_a, s_new = _gla(seq(gq), seq(gk), seq(gv), seq(la), s0)
    if k_past is None:
        o_b = _sb_prompt(seq(sq), seq(skb), seq(svb))
    else:
        o_b = _sb_sample(seq(sq), seq(skb), seq(svb), k_past, v_past)
    h2 = _merge_ln(h, o_a.reshape(m, GLA_V), gr, o_b.reshape(m, SB_W), ga, gb, *wts["merge"])
    y = _ffn_ln(h2, *wts["ffn2"])
    return (y.reshape(bsz, t, D_MODEL), s_new,
            sk.reshape(bsz, t, SB_HEADS, SB_DH), sv.reshape(bsz, t, SB_HEADS, SB_DH))


def kernel(x_prompt, x_sample, state_gla, cache_sb_k, cache_sb_v, ffn1_w_in, ffn1_w_out, ln1_g, ln1_b, w_in, w_gla_gate_up, b_gla_gate, g_gla_norm, w_gla_o, w_sb_o, w_out, ln2_g, ln2_b, ffn2_w_in, ffn2_w_out, ln3_g, ln3_b):
    xp, xs = x_prompt, x_sample
    bp = xp.shape[0]
    bs, _, _ = xs.shape
    outs = [[] for _ in range(6)]
    lr_lo = 2 * GLA_QK + 2 * GLA_V
    lr_hi = lr_lo + GLA_GATE_RANK
    for l in range(DEPTH):
        w_main = jnp.concatenate([w_in[l][:, :lr_lo], w_in[l][:, lr_hi:]], axis=1).astype(BF16)
        w_lr = jnp.pad(w_in[l][:, lr_lo:lr_hi], ((0, 0), (0, LANES - GLA_GATE_RANK))).astype(BF16)
        w_up = jnp.pad(w_gla_gate_up[l], ((0, LANES - GLA_GATE_RANK), (0, 0))).astype(BF16)
        wts = {
            "ffn1": (*_prep_ffn(ffn1_w_in[l], ffn1_w_out[l]), _row(ln1_g[l]), _row(ln1_b[l])),
            "proj": (w_main, w_lr, w_up, _row(b_gla_gate[l])),
            "merge": (_row(g_gla_norm[l]), w_gla_o[l].astype(BF16), w_sb_o[l].astype(BF16),
                      w_out[l].astype(BF16), _row(ln2_g[l]), _row(ln2_b[l])),
            "ffn2": (*_prep_ffn(ffn2_w_in[l], ffn2_w_out[l]), _row(ln3_g[l]), _row(ln3_b[l])),
        }
        s0_p = jnp.zeros((bp, GLA_HEADS, GLA_DK, GLA_DV), F32)
        xp, sp, kp, vp = _layer(xp, s0_p, None, None, wts)
        past = cache_sb_k.shape[2]
        xs, ss, ksn, vsn = _layer(xs, state_gla[l],
                                  cache_sb_k[l].reshape(bs, past, SB_W),
                                  cache_sb_v[l].reshape(bs, past, SB_W), wts)
        for lst, val in zip(outs, (sp, kp, vp, ss, ksn, vsn)):
            lst.append(val)
    return (xp, xs, *(jnp.stack(o) for o in outs))
```

```python
import functools

import jax
import jax.numpy as jnp
from jax import lax
from jax.experimental import pallas as pl
from jax.experimental.pallas import tpu as pltpu

F32 = jnp.float32
BF16 = jnp.bfloat16

D_MODEL = 1024
CHUNK = 64
GLA_HEADS = 4
GLA_DK = 128
GLA_DV = 256
GLA_QK = GLA_HEADS * GLA_DK
GLA_V = GLA_HEADS * GLA_DV
GLA_GATE_RANK = 16
GLA_GATE_TEMP = 16.0
SB_HEADS = 8
SB_DH = 128
SB_W = SB_HEADS * SB_DH
D_FF = 2816
LN_EPS = 1e-5
DEPTH = 1
DN_ALPHA = (2 * DEPTH) ** 0.25

LANES = 128
FF_CHUNK = 256
ROW_TILE = 512
PROJ_ROW_TILE = 256
SB_BLOCK = 256
SB_GROUP = 4
SB_DEAD = 110.0
VMEM_LIMIT = 56 * 1024 * 1024


def _layer_norm(r, g, b):
    mu = jnp.mean(r, axis=-1, keepdims=True)
    d = r - mu
    var = jnp.mean(d * d, axis=-1, keepdims=True)
    return d * lax.rsqrt(var + LN_EPS) * g + b


def _softplus_neg_abs(z):
    return jnp.log(1.0 + jnp.exp(-jnp.abs(z)))


def _split3(x):
    x1 = x.astype(BF16)
    r1 = x - x1.astype(F32)
    x2 = r1.astype(BF16)
    r2 = r1 - x2.astype(F32)
    return x1, x2, r2.astype(BF16)


def _dot(a, b):
    return jnp.dot(a, b, preferred_element_type=F32)


def _dot_nt(a, b):
    return lax.dot_general(a, b, (((1,), (1,)), ((), ())), preferred_element_type=F32)


def _dot_tn(a, b):
    return lax.dot_general(a, b, (((0,), (0,)), ((), ())), preferred_element_type=F32)


def _const_spec(shape):
    nd = len(shape)
    return pl.BlockSpec(shape, lambda *_: (0,) * nd, pipeline_mode=pl.Buffered(1))


def _params(*sem):
    return pltpu.CompilerParams(dimension_semantics=sem, vmem_limit_bytes=VMEM_LIMIT)


def _ffn_ln_kernel(x_ref, wg_ref, wu_ref, wo_ref, g_ref, b_ref, o_ref, acc_ref):
    x = x_ref[...]
    xb = x.astype(BF16)
    acc_ref[...] = jnp.zeros_like(acc_ref)

    def step(j, carry):
        gate = _dot(xb, wg_ref[j])
        up = _dot(xb, wu_ref[j])
        a = (gate * jax.nn.sigmoid(gate) * up).astype(BF16)
        acc_ref[...] += _dot(a, wo_ref[j])
        return carry

    lax.fori_loop(0, D_FF // FF_CHUNK, step, 0)
    r = DN_ALPHA * x + 0.5 * acc_ref[...]
    o_ref[...] = _layer_norm(r, g_ref[...], b_ref[...])


def _ffn_ln(x, wg, wu, wo, g, b):
    m = x.shape[0]
    tm = min(ROW_TILE, m)
    row = pl.BlockSpec((tm, D_MODEL), lambda i: (i, 0))
    return pl.pallas_call(
        _ffn_ln_kernel,
        grid=(m // tm,),
        in_specs=[row, _const_spec(wg.shape), _const_spec(wu.shape), _const_spec(wo.shape),
                  _const_spec(g.shape), _const_spec(b.shape)],
        out_specs=row,
        out_shape=jax.ShapeDtypeStruct((m, D_MODEL), F32),
        scratch_shapes=[pltpu.VMEM((tm, D_MODEL), F32)],
        compiler_params=_params("parallel"),
        name="ffn_ln",
    )(x, wg, wu, wo, g, b)


_PROJ_GROUPS = (("gq", GLA_QK), ("gk", GLA_QK), ("gv", GLA_V), ("gr", GLA_V),
                ("sq", SB_W), ("sk", SB_W), ("sv", SB_W), ("ga", D_MODEL), ("gb", D_MODEL))


def _proj_kernel(h_ref, w_ref, wlr_ref, wup_ref, bup_ref,
                 gq_ref, gk_ref, gv_ref, gr_ref, la_ref, sq_ref, sk_ref, sv_ref, ga_ref, gb_ref,
                 skb_ref, svb_ref):
    hb = h_ref[...].astype(BF16)
    outs = dict(gq=gq_ref, gk=gk_ref, gv=gv_ref, gr=gr_ref, sq=sq_ref, sk=sk_ref, sv=sv_ref,
                ga=ga_ref, gb=gb_ref)
    lo = 0
    for name, width in _PROJ_GROUPS:
        y = _dot(hb, w_ref[:, lo:lo + width])
        lo += width
        if name == "gq":
            y = y * (GLA_DK ** -0.5)
        if name == "sq":
            outs[name][...] = y.astype(BF16)
        elif name in ("sk", "sv"):
            for hd in range(SB_HEADS):
                outs[name][:, hd, :] = y[:, hd * SB_DH:(hd + 1) * SB_DH]
            (skb_ref if name == "sk" else svb_ref)[...] = y.astype(BF16)
        else:
            outs[name][...] = y
    lr = _dot(hb, wlr_ref[...]).astype(BF16)
    pre = _dot(lr, wup_ref[...]) + bup_ref[...]
    la_ref[...] = (jnp.minimum(pre, 0.0) - _softplus_neg_abs(pre)) * (1.0 / GLA_GATE_TEMP)


def _proj(h, w_main, w_lr, w_up, b_up):
    m = h.shape[0]
    tm = min(PROJ_ROW_TILE, m)

    def row(width):
        return pl.BlockSpec((tm, width), lambda i: (i, 0))

    def out(width, dtype=F32):
        return jax.ShapeDtypeStruct((m, width), dtype)

    widths = (GLA_QK, GLA_QK, GLA_V, GLA_V, GLA_QK, SB_W, SB_W, SB_W, D_MODEL, D_MODEL, SB_W, SB_W)
    dtypes = (F32, F32, F32, F32, F32, BF16, F32, F32, F32, F32, BF16, BF16)
    out_specs = [row(w) for w in widths]
    out_shape = [out(w, d) for w, d in zip(widths, dtypes)]
    for idx in (6, 7):
        out_specs[idx] = pl.BlockSpec((tm, SB_HEADS, SB_DH), lambda i: (i, 0, 0))
        out_shape[idx] = jax.ShapeDtypeStruct((m, SB_HEADS, SB_DH), F32)
    return pl.pallas_call(
        _proj_kernel,
        grid=(m // tm,),
        in_specs=[row(D_MODEL), _const_spec(w_main.shape), _const_spec(w_lr.shape),
                  _const_spec(w_up.shape), _const_spec(b_up.shape)],
        out_specs=out_specs,
        out_shape=out_shape,
        compiler_params=_params("parallel"),
        name="proj",
    )(h, w_main, w_lr, w_up, b_up)


def _gla_kernel(q_ref, k_ref, v_ref, la_ref, s0_ref, o_ref, sout_ref, s_scr, *, n_chunks):
    t = pl.program_id(1)

    @pl.when(t == 0)
    def _():
        s_scr[...] = s0_ref[0]

    rows = lax.broadcasted_iota(jnp.int32, (CHUNK, CHUNK), 0)
    cols = lax.broadcasted_iota(jnp.int32, (CHUNK, CHUNK), 1)
    causal = cols <= rows
    tri = causal.astype(BF16)

    def chunk(c, carry):
        r0 = pl.multiple_of(c * CHUNK, CHUNK)
        rs = pl.ds(r0, CHUNK)
        for h in range(GLA_HEADS):
            ks = slice(h * GLA_DK, (h + 1) * GLA_DK)
            vs = slice(h * GLA_DV, (h + 1) * GLA_DV)
            g = la_ref[0, rs, ks]
            g1, g2, g3 = _split3(g)
            b = _dot(tri, g1) + _dot(tri, g2) + _dot(tri, g3)
            b_last = b[CHUNK - 1:CHUNK, :]
            q = q_ref[0, rs, ks]
            k = k_ref[0, rs, ks]
            v = v_ref[0, rs, vs].astype(BF16)
            qg = (q * jnp.exp(b)).astype(BF16)
            kg = (k * jnp.exp(-b)).astype(BF16)
            kd = (k * jnp.exp(b_last - b)).astype(BF16)
            att = jnp.where(causal, _dot_nt(qg, kg), 0.0).astype(BF16)
            s_old = s_scr[h]
            o_ref[0, rs, vs] = _dot(att, v) + _dot(qg, s_old.astype(BF16))
            decay = jnp.exp(jnp.broadcast_to(b_last, (GLA_DK, GLA_DK))).T
            decay = jnp.concatenate([decay, decay], axis=1)
            s_scr[h] = decay * s_old + _dot_tn(kd, v)
        return carry

    lax.fori_loop(0, n_chunks, chunk, 0)

    @pl.when(t == pl.num_programs(1) - 1)
    def _():
        sout_ref[0] = s_scr[...]


def _gla(q, k, v, la, s0):
    bsz, t, _ = q.shape
    tb = min(512, t)
    qk_spec = pl.BlockSpec((1, tb, GLA_QK), lambda b, i: (b, i, 0))
    v_spec = pl.BlockSpec((1, tb, GLA_V), lambda b, i: (b, i, 0))
    s_spec = pl.BlockSpec((1, GLA_HEADS, GLA_DK, GLA_DV), lambda b, i: (b, 0, 0, 0))
    return pl.pallas_call(
        functools.partial(_gla_kernel, n_chunks=tb // CHUNK),
        grid=(bsz, t // tb),
        in_specs=[qk_spec, qk_spec, v_spec, qk_spec, s_spec],
        out_specs=[v_spec, s_spec],
        out_shape=[jax.ShapeDtypeStruct((bsz, t, GLA_V), F32),
                   jax.ShapeDtypeStruct((bsz, GLA_HEADS, GLA_DK, GLA_DV), F32)],
        scratch_shapes=[pltpu.VMEM((GLA_HEADS, GLA_DK, GLA_DV), F32)],
        compiler_params=_params("parallel", "arbitrary"),
        name="gla",
    )(q, k, v, la, s0)


def _suffix_matrix(tk):
    r = lax.broadcasted_iota(jnp.int32, (2 * tk, tk + LANES), 0) % tk
    c = lax.broadcasted_iota(jnp.int32, (2 * tk, tk + LANES), 1)
    return ((r >= c) | (c >= tk)).astype(BF16)


def _sb_update(q, k, v, carry_ref, acc_ref, hd, suffix, visible, first):
    tk = k.shape[0]
    z = _dot_nt(q, k) * (SB_DH ** -0.5)
    c = jnp.maximum(z, 0.0) + _softplus_neg_abs(z)
    if visible is not None:
        c = jnp.where(visible, c, 0.0)
    hi = c.astype(BF16)
    lo = (c - hi.astype(F32)).astype(BF16)
    sums = _dot(jnp.concatenate([hi, lo], axis=1), suffix)
    e = z - sums[:, :tk]
    if not first:
        carry = carry_ref[hd]
        e = e - (carry[:, :tk] if tk < LANES else jnp.concatenate([carry] * (tk // LANES), axis=1))
    w = jnp.exp(e)
    if visible is not None:
        w = jnp.where(visible, w, 0.0)
    pv = _dot(w.astype(BF16), v)
    if first:
        acc_ref[hd] = pv
        carry_ref[hd] = sums[:, tk:]
    else:
        acc_ref[hd] += pv
        carry_ref[hd] = carry + sums[:, tk:]


def _strictly_earlier(n):
    rows = lax.broadcasted_iota(jnp.int32, (n, n), 0)
    cols = lax.broadcasted_iota(jnp.int32, (n, n), 1)
    return cols < rows


def _head(hd):
    return slice(hd * SB_DH, (hd + 1) * SB_DH)


def _sb_prompt_kernel(q_ref, k_ref, v_ref, sfx_ref, o_ref, carry_scr, acc_scr):
    i = pl.program_id(2)
    blk = SB_BLOCK
    heads = q_ref.shape[2] // SB_DH
    sfx = sfx_ref[...]
    visible = _strictly_earlier(blk)
    r0 = pl.multiple_of(i * blk, blk)
    for hd in range(heads):
        _sb_update(q_ref[0, :, _head(hd)], k_ref[0, pl.ds(r0, blk), _head(hd)],
                   v_ref[0, pl.ds(r0, blk), _head(hd)], carry_scr, acc_scr, hd, sfx, visible, True)

    def cond(state):
        j, cmin = state
        return (j < i) & (cmin <= SB_DEAD)

    def body(state):
        j, _ = state
        k0 = pl.multiple_of((i - 1 - j) * blk, blk)
        for hd in range(heads):
            _sb_update(q_ref[0, :, _head(hd)], k_ref[0, pl.ds(k0, blk), _head(hd)],
                       v_ref[0, pl.ds(k0, blk), _head(hd)], carry_scr, acc_scr, hd, sfx, None, False)
        return j + 1, jnp.min(carry_scr[...])

    lax.while_loop(cond, body, (jnp.int32(0), jnp.min(carry_scr[...])))
    for hd in range(heads):
        o_ref[0, :, _head(hd)] = acc_scr[hd]


def _sb_prompt(sq, sk, sv):
    bsz, t, _ = sq.shape
    blk = SB_BLOCK
    assert t % blk == 0
    width = SB_GROUP * SB_DH
    sfx = _suffix_matrix(blk)
    q_spec = pl.BlockSpec((1, blk, width), lambda b, g, i: (b, i, g))
    kv_spec = pl.BlockSpec((1, t, width), lambda b, g, i: (b, 0, g), pipeline_mode=pl.Buffered(1))
    return pl.pallas_call(
        _sb_prompt_kernel,
        grid=(bsz, SB_HEADS // SB_GROUP, t // blk),
        in_specs=[q_spec, kv_spec, kv_spec, _const_spec(sfx.shape)],
        out_specs=q_spec,
        out_shape=jax.ShapeDtypeStruct((bsz, t, SB_W), F32),
        scratch_shapes=[pltpu.VMEM((SB_GROUP, blk, LANES), F32), pltpu.VMEM((SB_GROUP, blk, SB_DH), F32)],
        compiler_params=_params("parallel", "parallel", "arbitrary"),
        name="sb_prompt",
    )(sq, sk, sv, sfx)


def _sb_sample_kernel(q_ref, kn_ref, vn_ref, kw_ref, vw_ref, k_hbm, v_hbm, sfx_new_ref, sfx_ref, o_ref,
                      carry_scr, acc_scr, kbuf, vbuf, sem, *, n_past):
    b = pl.program_id(0)
    tq = q_ref.shape[1]
    blk = SB_BLOCK
    visible = _strictly_earlier(tq)
    sfx_new = sfx_new_ref[...]
    sfx = sfx_ref[...]
    for hd in range(SB_HEADS):
        _sb_update(q_ref[0, :, _head(hd)], kn_ref[0, :, _head(hd)], vn_ref[0, :, _head(hd)],
                   carry_scr, acc_scr, hd, sfx_new, visible, True)
    for hd in range(SB_HEADS):
        _sb_update(q_ref[0, :, _head(hd)], kw_ref[0, :, hd, :].astype(BF16), vw_ref[0, :, hd, :].astype(BF16),
                   carry_scr, acc_scr, hd, sfx, None, False)

    n_older = n_past // blk - 1

    def cond(state):
        j, cmin = state
        return (j < n_older) & (cmin <= SB_DEAD)

    def body(state):
        j, _ = state
        start = pl.multiple_of((n_older - 1 - j) * blk, blk)
        copy_k = pltpu.make_async_copy(k_hbm.at[b, pl.ds(start, blk)], kbuf, sem.at[0])
        copy_v = pltpu.make_async_copy(v_hbm.at[b, pl.ds(start, blk)], vbuf, sem.at[1])
        copy_k.start()
        copy_v.start()
        copy_k.wait()
        copy_v.wait()
        for hd in range(SB_HEADS):
            _sb_update(q_ref[0, :, _head(hd)], kbuf[:, hd, :].astype(BF16), vbuf[:, hd, :].astype(BF16),
                       carry_scr, acc_scr, hd, sfx, None, False)
        return j + 1, jnp.min(carry_scr[...])

    lax.while_loop(cond, body, (jnp.int32(0), jnp.min(carry_scr[...])))
    for hd in range(SB_HEADS):
        o_ref[0, :, _head(hd)] = acc_scr[hd]


def _sb_sample(sq, sk, sv, k_past, v_past):
    bsz, t, _ = sq.shape
    p = k_past.shape[1]
    blk = SB_BLOCK
    assert p % blk == 0 and p >= blk
    sfx_new = _suffix_matrix(t)
    sfx = _suffix_matrix(blk)
    new_spec = pl.BlockSpec((1, t, SB_W), lambda b: (b, 0, 0))
    window_spec = pl.BlockSpec((1, blk, SB_HEADS, SB_DH), lambda b: (b, p // blk - 1, 0, 0))
    hbm_spec = pl.BlockSpec(memory_space=pl.ANY)
    return pl.pallas_call(
        functools.partial(_sb_sample_kernel, n_past=p),
        grid=(bsz,),
        in_specs=[new_spec, new_spec, new_spec, window_spec, window_spec, hbm_spec, hbm_spec,
                  _const_spec(sfx_new.shape), _const_spec(sfx.shape)],
        out_specs=new_spec,
        out_shape=jax.ShapeDtypeStruct((bsz, t, SB_W), F32),
        scratch_shapes=[pltpu.VMEM((SB_HEADS, t, LANES), F32), pltpu.VMEM((SB_HEADS, t, SB_DH), F32),
                        pltpu.VMEM((blk, SB_HEADS, SB_DH), F32), pltpu.VMEM((blk, SB_HEADS, SB_DH), F32),
                        pltpu.SemaphoreType.DMA((2,))],
        compiler_params=_params("arbitrary"),
        name="sb_sample",
    )(sq, sk, sv, k_past, v_past, k_past, v_past, sfx_new, sfx)


def _merge_ln_kernel(h_ref, oa_ref, gr_ref, ob_ref, ga_ref, gb_ref, gn_ref,
                     wa_ref, wb_ref, wo_ref, g_ref, b_ref, o_ref):
    o = oa_ref[...]
    gn = gn_ref[...]
    parts = []
    for hd in range(GLA_HEADS):
        vs = slice(hd * GLA_DV, (hd + 1) * GLA_DV)
        oh = o[:, vs]
        ms = jnp.mean(oh * oh, axis=-1, keepdims=True)
        parts.append(oh * lax.rsqrt(ms + LN_EPS) * gn[:, vs])
    gr = gr_ref[...]
    oa = jnp.concatenate(parts, axis=1) * (gr * jax.nn.sigmoid(gr))
    branch_a = _dot(oa.astype(BF16), wa_ref[...])
    branch_b = _dot(ob_ref[...].astype(BF16), wb_ref[...])
    merged = jax.nn.sigmoid(ga_ref[...]) * branch_a + jax.nn.sigmoid(gb_ref[...]) * branch_b
    mix = _dot(merged.astype(BF16), wo_ref[...])
    o_ref[...] = _layer_norm(DN_ALPHA * h_ref[...] + mix, g_ref[...], b_ref[...])


def _merge_ln(h, oa, gr, ob, ga, gb, gn, wa, wb, wo, g, b):
    m = h.shape[0]
    tm = min(ROW_TILE, m)
    row = pl.BlockSpec((tm, D_MODEL), lambda i: (i, 0))
    consts = [gn, wa, wb, wo, g, b]
    return pl.pallas_call(
        _merge_ln_kernel,
        grid=(m // tm,),
        in_specs=[row] * 6 + [_const_spec(c.shape) for c in consts],
        out_specs=row,
        out_shape=jax.ShapeDtypeStruct((m, D_MODEL), F32),
        compiler_params=_params("parallel"),
        name="merge_ln",
    )(h, oa, gr, ob, ga, gb, *consts)


def _prep_ffn(w_in, w_out):
    n = D_FF // FF_CHUNK
    wg = w_in[:, :D_FF].reshape(D_MODEL, n, FF_CHUNK).transpose(1, 0, 2).astype(BF16)
    wu = w_in[:, D_FF:].reshape(D_MODEL, n, FF_CHUNK).transpose(1, 0, 2).astype(BF16)
    wo = w_out.reshape(n, FF_CHUNK, D_MODEL).astype(BF16)
    return wg, wu, wo


def _row(v):
    return v.reshape(1, -1)


def _layer(x, s0, k_past, v_past, wts):
    bsz, t, _ = x.shape
    m = bsz * t
    h = _ffn_ln(x.reshape(m, D_MODEL), *wts["ffn1"])
    gq, gk, gv, gr, la, sq, sk, sv, ga, gb, skb, svb = _proj(h, *wts["proj"])

    def seq(a):
        return a.reshape(bsz, t, a.shape[-1])

    o_a, s_new = _gla(seq(gq), seq(gk), seq(gv), seq(la), s0)
    if k_past is None:
        o_b = _sb_prompt(seq(sq), seq(skb), seq(svb))
    else:
        o_b = _sb_sample(seq(sq), seq(skb), seq(svb), k_past, v_past)
    h2 = _merge_ln(h, o_a.reshape(m, GLA_V), gr, o_b.reshape(m, SB_W), ga, gb, *wts["merge"])
    y = _ffn_ln(h2, *wts["ffn2"])
    return (y.reshape(bsz, t, D_MODEL), s_new,
            sk.reshape(bsz, t, SB_HEADS, SB_DH), sv.reshape(bsz, t, SB_HEADS, SB_DH))


def kernel(x_prompt, x_sample, state_gla, cache_sb_k, cache_sb_v, ffn1_w_in, ffn1_w_out, ln1_g, ln1_b, w_in, w_gla_gate_up, b_gla_gate, g_gla_norm, w_gla_o, w_sb_o, w_out, ln2_g, ln2_b, ffn2_w_in, ffn2_w_out, ln3_g, ln3_b):
    xp, xs = x_prompt, x_sample
    bp = xp.shape[0]
    outs = [[] for _ in range(6)]
    lr_lo = 2 * GLA_QK + 2 * GLA_V
    lr_hi = lr_lo + GLA_GATE_RANK
    for l in range(DEPTH):
        w_main = jnp.concatenate([w_in[l][:, :lr_lo], w_in[l][:, lr_hi:]], axis=1).astype(BF16)
        w_lr = jnp.pad(w_in[l][:, lr_lo:lr_hi], ((0, 0), (0, LANES - GLA_GATE_RANK))).astype(BF16)
        w_up = jnp.pad(w_gla_gate_up[l], ((0, LANES - GLA_GATE_RANK), (0, 0))).astype(BF16)
        wts = {
            "ffn1": (*_prep_ffn(ffn1_w_in[l], ffn1_w_out[l]), _row(ln1_g[l]), _row(ln1_b[l])),
            "proj": (w_main, w_lr, w_up, _row(b_gla_gate[l])),
            "merge": (_row(g_gla_norm[l]), w_gla_o[l].astype(BF16), w_sb_o[l].astype(BF16),
                      w_out[l].astype(BF16), _row(ln2_g[l]), _row(ln2_b[l])),
            "ffn2": (*_prep_ffn(ffn2_w_in[l], ffn2_w_out[l]), _row(ln3_g[l]), _row(ln3_b[l])),
        }
        s0_p = jnp.zeros((bp, GLA_HEADS, GLA_DK, GLA_DV), F32)
        xp, sp, kp, vp = _layer(xp, s0_p, None, None, wts)
        xs, ss, ksn, vsn = _layer(xs, state_gla[l], cache_sb_k[l], cache_sb_v[l], wts)
        for lst, val in zip(outs, (sp, kp, vp, ss, ksn, vsn)):
            lst.append(val)
    return (xp, xs, *(jnp.stack(o) for o in outs))
```

```python
import functools

import jax
import jax.numpy as jnp
from jax import lax
from jax.experimental import pallas as pl
from jax.experimental.pallas import tpu as pltpu

F32 = jnp.float32
BF16 = jnp.bfloat16

D_MODEL = 1024
CHUNK = 64
GLA_HEADS = 4
GLA_DK = 128
GLA_DV = 256
GLA_QK = GLA_HEADS * GLA_DK
GLA_V = GLA_HEADS * GLA_DV
GLA_GATE_RANK = 16
GLA_GATE_TEMP = 16.0
SB_HEADS = 8
SB_DH = 128
SB_W = SB_HEADS * SB_DH
D_FF = 2816
LN_EPS = 1e-5
DEPTH = 1
DN_ALPHA = (2 * DEPTH) ** 0.25

LANES = 128
GLA_ROWS = 256
ROW_TILE = 512
PROJ_ROW_TILE = 256
SB_BLOCK = 256
SB_GROUP = 8
SB_DEAD = 110.0
VMEM_LIMIT = 56 * 1024 * 1024


def _layer_norm(r, g, b):
    mu = jnp.mean(r, axis=-1, keepdims=True)
    d = r - mu
    var = jnp.mean(d * d, axis=-1, keepdims=True)
    return d * lax.rsqrt(var + LN_EPS) * g + b


def _softplus_neg_abs(z):
    return jnp.log(1.0 + jnp.exp(-jnp.abs(z)))


def _split3(x):
    x1 = x.astype(BF16)
    r1 = x - x1.astype(F32)
    x2 = r1.astype(BF16)
    r2 = r1 - x2.astype(F32)
    return x1, x2, r2.astype(BF16)


def _dot(a, b):
    return jnp.dot(a, b, preferred_element_type=F32)


def _dot_nt(a, b):
    return lax.dot_general(a, b, (((1,), (1,)), ((), ())), preferred_element_type=F32)


def _dot_tn(a, b):
    return lax.dot_general(a, b, (((0,), (0,)), ((), ())), preferred_element_type=F32)


def _const_spec(shape):
    nd = len(shape)
    return pl.BlockSpec(shape, lambda *_: (0,) * nd, pipeline_mode=pl.Buffered(1))


def _params(*sem):
    return pltpu.CompilerParams(dimension_semantics=sem, vmem_limit_bytes=VMEM_LIMIT)


def _ffn_ln_kernel(x_ref, wg_ref, wu_ref, wo_ref, g_ref, b_ref, o_ref):
    x = x_ref[...]
    xb = x.astype(BF16)
    gate = _dot(xb, wg_ref[...])
    up = _dot(xb, wu_ref[...])
    a = (gate * jax.nn.sigmoid(gate) * up).astype(BF16)
    r = DN_ALPHA * x + 0.5 * _dot(a, wo_ref[...])
    o_ref[...] = _layer_norm(r, g_ref[...], b_ref[...])


def _ffn_ln(x, wg, wu, wo, g, b):
    m = x.shape[0]
    tm = min(ROW_TILE, m)
    row = pl.BlockSpec((tm, D_MODEL), lambda i: (i, 0))
    return pl.pallas_call(
        _ffn_ln_kernel,
        grid=(m // tm,),
        in_specs=[row, _const_spec(wg.shape), _const_spec(wu.shape), _const_spec(wo.shape),
                  _const_spec(g.shape), _const_spec(b.shape)],
        out_specs=row,
        out_shape=jax.ShapeDtypeStruct((m, D_MODEL), F32),
        compiler_params=_params("parallel"),
        name="ffn_ln",
    )(x, wg, wu, wo, g, b)


_PROJ_GROUPS = (("gq", GLA_QK), ("gk", GLA_QK), ("gv", GLA_V), ("gr", GLA_V),
                ("sq", SB_W), ("sk", SB_W), ("sv", SB_W), ("ga", D_MODEL), ("gb", D_MODEL))


def _proj_kernel(h_ref, w_ref, wlr_ref, wup_ref, bup_ref,
                 gq_ref, gk_ref, gv_ref, gr_ref, la_ref, sq_ref, sk_ref, sv_ref, ga_ref, gb_ref,
                 skb_ref, svb_ref):
    hb = h_ref[...].astype(BF16)
    outs = dict(gq=gq_ref, gk=gk_ref, gv=gv_ref, gr=gr_ref, sq=sq_ref, sk=sk_ref, sv=sv_ref,
                ga=ga_ref, gb=gb_ref)
    lo = 0
    y_all = _dot(hb, w_ref[...])
    for name, width in _PROJ_GROUPS:
        y = y_all[:, lo:lo + width]
        lo += width
        if name == "gq":
            y = y * (GLA_DK ** -0.5)
        if name in ("sq", "gv"):
            outs[name][...] = y.astype(BF16)
        elif name in ("sk", "sv"):
            for hd in range(SB_HEADS):
                outs[name][:, hd, :] = y[:, hd * SB_DH:(hd + 1) * SB_DH]
            (skb_ref if name == "sk" else svb_ref)[...] = y.astype(BF16)
        else:
            outs[name][...] = y
    lr = _dot(hb, wlr_ref[...]).astype(BF16)
    pre = _dot(lr, wup_ref[...]) + bup_ref[...]
    la_ref[...] = (jnp.minimum(pre, 0.0) - _softplus_neg_abs(pre)) * (1.0 / GLA_GATE_TEMP)


def _proj(h, w_main, w_lr, w_up, b_up):
    m = h.shape[0]
    tm = min(PROJ_ROW_TILE, m)

    def row(width):
        return pl.BlockSpec((tm, width), lambda i: (i, 0))

    def out(width, dtype=F32):
        return jax.ShapeDtypeStruct((m, width), dtype)

    widths = (GLA_QK, GLA_QK, GLA_V, GLA_V, GLA_QK, SB_W, SB_W, SB_W, D_MODEL, D_MODEL, SB_W, SB_W)
    dtypes = (F32, F32, BF16, F32, F32, BF16, F32, F32, F32, F32, BF16, BF16)
    out_specs = [row(w) for w in widths]
    out_shape = [out(w, d) for w, d in zip(widths, dtypes)]
    for idx in (6, 7):
        out_specs[idx] = pl.BlockSpec((tm, SB_HEADS, SB_DH), lambda i: (i, 0, 0))
        out_shape[idx] = jax.ShapeDtypeStruct((m, SB_HEADS, SB_DH), F32)
    return pl.pallas_call(
        _proj_kernel,
        grid=(m // tm,),
        in_specs=[row(D_MODEL), _const_spec(w_main.shape), _const_spec(w_lr.shape),
                  _const_spec(w_up.shape), _const_spec(b_up.shape)],
        out_specs=out_specs,
        out_shape=out_shape,
        compiler_params=_params("parallel"),
        name="proj",
    )(h, w_main, w_lr, w_up, b_up)


def _gla_cumsum_matrix(tb):
    r = lax.broadcasted_iota(jnp.int32, (2 * tb, tb), 0)
    c = lax.broadcasted_iota(jnp.int32, (2 * tb, tb), 1)
    same_chunk = (r % tb) // CHUNK == c // CHUNK
    return (same_chunk & ((c <= r) | (r >= tb))).astype(BF16)


def _gla_kernel(q_ref, k_ref, v_ref, la_ref, s0_ref, cm_ref, o_ref, sout_ref, s_scr):
    t = pl.program_id(1)

    @pl.when(t == 0)
    def _():
        s_scr[...] = s0_ref[...]

    nb, tb, _ = q_ref.shape
    cm = cm_ref[...]
    rows = lax.broadcasted_iota(jnp.int32, (tb, tb), 0)
    cols = lax.broadcasted_iota(jnp.int32, (tb, tb), 1)
    shift = CHUNK.bit_length() - 1
    causal = (cols <= rows) & ((rows >> shift) == (cols >> shift))

    for bi in range(nb):
        for h in range(GLA_HEADS):
            ks = slice(h * GLA_DK, (h + 1) * GLA_DK)
            vs = slice(h * GLA_DV, (h + 1) * GLA_DV)
            cs = _dot(cm, jnp.concatenate(_split3(la_ref[bi, :, ks]), axis=1))
            cs = cs[:, :GLA_DK] + cs[:, GLA_DK:2 * GLA_DK] + cs[:, 2 * GLA_DK:]
            b, b_tot = cs[:tb], cs[tb:]
            q = q_ref[bi, :, ks]
            k = k_ref[bi, :, ks]
            v = v_ref[bi, :, vs]
            qg = (q * jnp.exp(b)).astype(BF16)
            kg = (k * jnp.exp(-b)).astype(BF16)
            kd = (k * jnp.exp(b_tot - b)).astype(BF16)
            att = jnp.where(causal, _dot_nt(qg, kg), 0.0).astype(BF16)
            o_intra = _dot(att, v)
            s = s_scr[bi, h]
            for c in range(tb // CHUNK):
                rs = slice(c * CHUNK, (c + 1) * CHUNK)
                o_ref[bi, rs, vs] = o_intra[rs] + _dot(qg[rs], s.astype(BF16))
                decay = jnp.exp(jnp.broadcast_to(b_tot[c * CHUNK:c * CHUNK + 1], (GLA_DK, GLA_DK))).T
                s = jnp.concatenate([decay, decay], axis=1) * s + _dot_tn(kd[rs], v[rs])
            s_scr[bi, h] = s

    @pl.when(t == pl.num_programs(1) - 1)
    def _():
        sout_ref[...] = s_scr[...]


def _gla(q, k, v, la, s0):
    bsz, t, _ = q.shape
    tb = min(GLA_ROWS, t)
    nb = max(1, min(bsz, GLA_ROWS // t))
    assert t % tb == 0 and bsz % nb == 0 and tb % CHUNK == 0
    cm = _gla_cumsum_matrix(tb)
    qk_spec = pl.BlockSpec((nb, tb, GLA_QK), lambda b, i: (b, i, 0))
    v_spec = pl.BlockSpec((nb, tb, GLA_V), lambda b, i: (b, i, 0))
    s_spec = pl.BlockSpec((nb, GLA_HEADS, GLA_DK, GLA_DV), lambda b, i: (b, 0, 0, 0))
    return pl.pallas_call(
        _gla_kernel,
        grid=(bsz // nb, t // tb),
        in_specs=[qk_spec, qk_spec, v_spec, qk_spec, s_spec, _const_spec(cm.shape)],
        out_specs=[v_spec, s_spec],
        out_shape=[jax.ShapeDtypeStruct((bsz, t, GLA_V), F32),
                   jax.ShapeDtypeStruct((bsz, GLA_HEADS, GLA_DK, GLA_DV), F32)],
        scratch_shapes=[pltpu.VMEM((nb, GLA_HEADS, GLA_DK, GLA_DV), F32)],
        compiler_params=_params("parallel", "arbitrary"),
        name="gla",
    )(q, k, v, la, s0, cm)


def _suffix_matrix(tk):
    r = lax.broadcasted_iota(jnp.int32, (2 * tk, tk + LANES), 0) % tk
    c = lax.broadcasted_iota(jnp.int32, (2 * tk, tk + LANES), 1)
    return ((r >= c) | (c >= tk)).astype(BF16)


def _sb_update(q, k, v, carry_ref, acc_ref, hd, suffix, visible, first):
    tk = k.shape[0]
    z = _dot_nt(q, k) * (SB_DH ** -0.5)
    c = jnp.maximum(z, 0.0) + _softplus_neg_abs(z)
    if visible is not None:
        c = jnp.where(visible, c, 0.0)
    hi = c.astype(BF16)
    lo = (c - hi.astype(F32)).astype(BF16)
    sums = _dot(jnp.concatenate([hi, lo], axis=1), suffix)
    e = z - sums[:, :tk]
    total = sums[:, tk:]
    if not first:
        carry = carry_ref[hd]
        e = e - (carry[:, :tk] if tk < LANES else jnp.concatenate([carry] * (tk // LANES), axis=1))
    w = jnp.exp(e)
    if visible is not None:
        w = jnp.where(visible, w, 0.0)
    pv = _dot(w.astype(BF16), v)
    if first:
        acc_ref[hd] = pv
        carry_ref[hd] = total
    else:
        acc_ref[hd] += pv
        carry_ref[hd] = carry + total


def _strictly_earlier(n):
    rows = lax.broadcasted_iota(jnp.int32, (n, n), 0)
    cols = lax.broadcasted_iota(jnp.int32, (n, n), 1)
    return cols < rows


def _head(hd):
    return slice(hd * SB_DH, (hd + 1) * SB_DH)


def _sb_prompt_kernel(q_ref, k_ref, v_ref, sfx_ref, o_ref, carry_scr, acc_scr):
    i = pl.program_id(2)
    blk = SB_BLOCK
    heads = q_ref.shape[2] // SB_DH
    sfx = sfx_ref[...]
    visible = _strictly_earlier(blk)
    r0 = pl.multiple_of(i * blk, blk)
    for hd in range(heads):
        _sb_update(q_ref[0, :, _head(hd)], k_ref[0, pl.ds(r0, blk), _head(hd)],
                   v_ref[0, pl.ds(r0, blk), _head(hd)], carry_scr, acc_scr, hd, sfx, visible, True)

    def cond(state):
        j, cmin = state
        return (j < i) & (cmin <= SB_DEAD)

    def body(state):
        j, _ = state
        k0 = pl.multiple_of((i - 1 - j) * blk, blk)
        for hd in range(heads):
            _sb_update(q_ref[0, :, _head(hd)], k_ref[0, pl.ds(k0, blk), _head(hd)],
                       v_ref[0, pl.ds(k0, blk), _head(hd)], carry_scr, acc_scr, hd, sfx, None, False)
        return j + 1, jnp.min(carry_scr[...])

    lax.while_loop(cond, body, (jnp.int32(0), jnp.min(carry_scr[...])))
    for hd in range(heads):
        o_ref[0, :, _head(hd)] = acc_scr[hd].astype(o_ref.dtype)


def _sb_prompt(sq, sk, sv):
    bsz, t, _ = sq.shape
    blk = SB_BLOCK
    assert t % blk == 0
    width = SB_GROUP * SB_DH
    sfx = _suffix_matrix(blk)
    q_spec = pl.BlockSpec((1, blk, width), lambda b, g, i: (b, i, g))
    kv_spec = pl.BlockSpec((1, t, width), lambda b, g, i: (b, 0, g), pipeline_mode=pl.Buffered(1))
    return pl.pallas_call(
        _sb_prompt_kernel,
        grid=(bsz, SB_HEADS // SB_GROUP, t // blk),
        in_specs=[q_spec, kv_spec, kv_spec, _const_spec(sfx.shape)],
        out_specs=q_spec,
        out_shape=jax.ShapeDtypeStruct((bsz, t, SB_W), BF16),
        scratch_shapes=[pltpu.VMEM((SB_GROUP, blk, LANES), F32), pltpu.VMEM((SB_GROUP, blk, SB_DH), F32)],
        compiler_params=_params("parallel", "parallel", "arbitrary"),
        name="sb_prompt",
    )(sq, sk, sv, sfx)


def _sb_sample_kernel(q_ref, kn_ref, vn_ref, kw_ref, vw_ref, k_hbm, v_hbm, sfx_new_ref, sfx_ref, o_ref,
                      carry_scr, acc_scr, kbuf, vbuf, sem, *, n_past):
    b = pl.program_id(0)
    tq = q_ref.shape[1]
    blk = SB_BLOCK
    visible = _strictly_earlier(tq)
    sfx_new = sfx_new_ref[...]
    sfx = sfx_ref[...]
    for hd in range(SB_HEADS):
        _sb_update(q_ref[0, :, _head(hd)], kn_ref[0, :, _head(hd)], vn_ref[0, :, _head(hd)],
                   carry_scr, acc_scr, hd, sfx_new, visible, True)
    def head_rows(ref, hd):
        return ref[pl.ds(hd, blk, stride=SB_HEADS), :].astype(BF16)

    for hd in range(SB_HEADS):
        _sb_update(q_ref[0, :, _head(hd)], head_rows(kw_ref.at[0], hd), head_rows(vw_ref.at[0], hd),
                   carry_scr, acc_scr, hd, sfx, None, False)

    n_older = n_past // blk - 1

    def cond(state):
        j, cmin = state
        return (j < n_older) & (cmin <= SB_DEAD)

    def body(state):
        j, _ = state
        start = pl.multiple_of((n_older - 1 - j) * blk * SB_HEADS, blk * SB_HEADS)
        copy_k = pltpu.make_async_copy(k_hbm.at[b, pl.ds(start, blk * SB_HEADS)], kbuf, sem.at[0])
        copy_v = pltpu.make_async_copy(v_hbm.at[b, pl.ds(start, blk * SB_HEADS)], vbuf, sem.at[1])
        copy_k.start()
        copy_v.start()
        copy_k.wait()
        copy_v.wait()
        for hd in range(SB_HEADS):
            _sb_update(q_ref[0, :, _head(hd)], head_rows(kbuf, hd), head_rows(vbuf, hd),
                       carry_scr, acc_scr, hd, sfx, None, False)
        return j + 1, jnp.min(carry_scr[...])

    lax.while_loop(cond, body, (jnp.int32(0), jnp.min(carry_scr[...])))
    for hd in range(SB_HEADS):
        o_ref[0, :, _head(hd)] = acc_scr[hd].astype(o_ref.dtype)


def _sb_sample(sq, sk, sv, k_past, v_past):
    bsz, t, _ = sq.shape
    p = k_past.shape[1]
    blk = SB_BLOCK
    assert p % blk == 0 and p >= blk
    k_past = k_past.reshape(bsz, p * SB_HEADS, SB_DH)
    v_past = v_past.reshape(bsz, p * SB_HEADS, SB_DH)
    sfx_new = _suffix_matrix(t)
    sfx = _suffix_matrix(blk)
    new_spec = pl.BlockSpec((1, t, SB_W), lambda b: (b, 0, 0))
    window_spec = pl.BlockSpec((1, blk * SB_HEADS, SB_DH), lambda b: (b, p // blk - 1, 0))
    hbm_spec = pl.BlockSpec(memory_space=pl.ANY)
    return pl.pallas_call(
        functools.partial(_sb_sample_kernel, n_past=p),
        grid=(bsz,),
        in_specs=[new_spec, new_spec, new_spec, window_spec, window_spec, hbm_spec, hbm_spec,
                  _const_spec(sfx_new.shape), _const_spec(sfx.shape)],
        out_specs=new_spec,
        out_shape=jax.ShapeDtypeStruct((bsz, t, SB_W), BF16),
        scratch_shapes=[pltpu.VMEM((SB_HEADS, t, LANES), F32), pltpu.VMEM((SB_HEADS, t, SB_DH), F32),
                        pltpu.VMEM((blk * SB_HEADS, SB_DH), F32), pltpu.VMEM((blk * SB_HEADS, SB_DH), F32),
                        pltpu.SemaphoreType.DMA((2,))],
        compiler_params=_params("arbitrary"),
        name="sb_sample",
    )(sq, sk, sv, k_past, v_past, k_past, v_past, sfx_new, sfx)


def _merge_ln_kernel(h_ref, oa_ref, gr_ref, ob_ref, ga_ref, gb_ref, gn_ref,
                     wa_ref, wb_ref, wo_ref, g_ref, b_ref, o_ref):
    o = oa_ref[...]
    gn = gn_ref[...]
    parts = []
    for hd in range(GLA_HEADS):
        vs = slice(hd * GLA_DV, (hd + 1) * GLA_DV)
        oh = o[:, vs]
        ms = jnp.mean(oh * oh, axis=-1, keepdims=True)
        parts.append(oh * lax.rsqrt(ms + LN_EPS) * gn[:, vs])
    gr = gr_ref[...]
    oa = jnp.concatenate(parts, axis=1) * (gr * jax.nn.sigmoid(gr))
    branch_a = _dot(oa.astype(BF16), wa_ref[...])
    branch_b = _dot(ob_ref[...], wb_ref[...])
    merged = jax.nn.sigmoid(ga_ref[...]) * branch_a + jax.nn.sigmoid(gb_ref[...]) * branch_b
    mix = _dot(merged.astype(BF16), wo_ref[...])
    o_ref[...] = _layer_norm(DN_ALPHA * h_ref[...] + mix, g_ref[...], b_ref[...])


def _merge_ln(h, oa, gr, ob, ga, gb, gn, wa, wb, wo, g, b):
    m = h.shape[0]
    tm = min(ROW_TILE, m)
    row = pl.BlockSpec((tm, D_MODEL), lambda i: (i, 0))
    consts = [gn, wa, wb, wo, g, b]
    return pl.pallas_call(
        _merge_ln_kernel,
        grid=(m // tm,),
        in_specs=[row] * 6 + [_const_spec(c.shape) for c in consts],
        out_specs=row,
        out_shape=jax.ShapeDtypeStruct((m, D_MODEL), F32),
        compiler_params=_params("parallel"),
        name="merge_ln",
    )(h, oa, gr, ob, ga, gb, *consts)


def _prep_ffn(w_in, w_out):
    return w_in[:, :D_FF].astype(BF16), w_in[:, D_FF:].astype(BF16), w_out.astype(BF16)


def _row(v):
    return v.reshape(1, -1)


def _layer(x, s0, k_past, v_past, wts):
    bsz, t, _ = x.shape
    m = bsz * t
    h = _ffn_ln(x.reshape(m, D_MODEL), *wts["ffn1"])
    gq, gk, gv, gr, la, sq, sk, sv, ga, gb, skb, svb = _proj(h, *wts["proj"])

    def seq(a):
        return a.reshape(bsz, t, a.shape[-1])

    o_a, s_new = _gla(seq(gq), seq(gk), seq(gv), seq(la), s0)
    if k_past is None:
        o_b = _sb_prompt(seq(sq), seq(skb), seq(svb))
    else:
        o_b = _sb_sample(seq(sq), seq(skb), seq(svb), k_past, v_past)
    h2 = _merge_ln(h, o_a.reshape(m, GLA_V), gr, o_b.reshape(m, SB_W), ga, gb, *wts["merge"])
    y = _ffn_ln(h2, *wts["ffn2"])
    return (y.reshape(bsz, t, D_MODEL), s_new,
            sk.reshape(bsz, t, SB_HEADS, SB_DH), sv.reshape(bsz, t, SB_HEADS, SB_DH))


def kernel(x_prompt, x_sample, state_gla, cache_sb_k, cache_sb_v, ffn1_w_in, ffn1_w_out, ln1_g, ln1_b, w_in, w_gla_gate_up, b_gla_gate, g_gla_norm, w_gla_o, w_sb_o, w_out, ln2_g, ln2_b, ffn2_w_in, ffn2_w_out, ln3_g, ln3_b):
    xp, xs = x_prompt, x_sample
    bp = xp.shape[0]
    outs = [[] for _ in range(6)]
    lr_lo = 2 * GLA_QK + 2 * GLA_V
    lr_hi = lr_lo + GLA_GATE_RANK
    for l in range(DEPTH):
        w_main = jnp.concatenate([w_in[l][:, :lr_lo], w_in[l][:, lr_hi:]], axis=1).astype(BF16)
        w_lr = jnp.pad(w_in[l][:, lr_lo:lr_hi], ((0, 0), (0, LANES - GLA_GATE_RANK))).astype(BF16)
        w_up = jnp.pad(w_gla_gate_up[l], ((0, LANES - GLA_GATE_RANK), (0, 0))).astype(BF16)
        wts = {
            "ffn1": (*_prep_ffn(ffn1_w_in[l], ffn1_w_out[l]), _row(ln1_g[l]), _row(ln1_b[l])),
            "proj": (w_main, w_lr, w_up, _row(b_gla_gate[l])),
            "merge": (_row(g_gla_norm[l]), w_gla_o[l].astype(BF16), w_sb_o[l].astype(BF16),
                      w_out[l].astype(BF16), _row(ln2_g[l]), _row(ln2_b[l])),
            "ffn2": (*_prep_ffn(ffn2_w_in[l], ffn2_w_out[l]), _row(ln3_g[l]), _row(ln3_b[l])),
        }
        s0_p = jnp.zeros((bp, GLA_HEADS, GLA_DK, GLA_DV), F32)
        xp, sp, kp, vp = _layer(xp, s0_p, None, None, wts)
        xs, ss, ksn, vsn = _layer(xs, state_gla[l], cache_sb_k[l], cache_sb_v[l], wts)
        for lst, val in zip(outs, (sp, kp, vp, ss, ksn, vsn)):
            lst.append(val)
    return (xp, xs, *(jnp.stack(o) for o in outs))
```

```python
import functools

import jax
import jax.numpy as jnp
from jax import lax
from jax.experimental import pallas as pl
from jax.experimental.pallas import tpu as pltpu

F32 = jnp.float32
BF16 = jnp.bfloat16

D_MODEL = 1024
CHUNK = 64
GLA_HEADS = 4
GLA_DK = 128
GLA_DV = 256
GLA_QK = GLA_HEADS * GLA_DK
GLA_V = GLA_HEADS * GLA_DV
GLA_GATE_RANK = 16
GLA_GATE_TEMP = 16.0
SB_HEADS = 8
SB_DH = 128
SB_W = SB_HEADS * SB_DH
D_FF = 2816
LN_EPS = 1e-5
DEPTH = 1
DN_ALPHA = (2 * DEPTH) ** 0.25

LANES = 128
GLA_ROWS = 256
ROW_TILE = 512
PROJ_ROW_TILE = 256
SB_BLOCK = 256
SB_GROUP = 8
SB_DEAD = 110.0
VMEM_LIMIT = 56 * 1024 * 1024


def _layer_norm(r, g, b):
    mu = jnp.mean(r, axis=-1, keepdims=True)
    d = r - mu
    var = jnp.mean(d * d, axis=-1, keepdims=True)
    return d * lax.rsqrt(var + LN_EPS) * g + b


def _softplus_neg_abs(z):
    return jnp.log(1.0 + jnp.exp(-jnp.abs(z)))


def _split3(x):
    x1 = x.astype(BF16)
    r1 = x - x1.astype(F32)
    x2 = r1.astype(BF16)
    r2 = r1 - x2.astype(F32)
    return x1, x2, r2.astype(BF16)


def _dot(a, b):
    return jnp.dot(a, b, preferred_element_type=F32)


def _dot_nt(a, b):
    return lax.dot_general(a, b, (((1,), (1,)), ((), ())), preferred_element_type=F32)


def _dot_tn(a, b):
    return lax.dot_general(a, b, (((0,), (0,)), ((), ())), preferred_element_type=F32)


def _const_spec(shape):
    nd = len(shape)
    return pl.BlockSpec(shape, lambda *_: (0,) * nd, pipeline_mode=pl.Buffered(1))


def _params(*sem):
    return pltpu.CompilerParams(dimension_semantics=sem, vmem_limit_bytes=VMEM_LIMIT)


def _ffn_ln_kernel(x_ref, wg_ref, wu_ref, wo_ref, g_ref, b_ref, o_ref):
    x = x_ref[...]
    xb = x.astype(BF16)
    gate = _dot(xb, wg_ref[...])
    up = _dot(xb, wu_ref[...])
    a = (gate * jax.nn.sigmoid(gate) * up).astype(BF16)
    r = DN_ALPHA * x + 0.5 * _dot(a, wo_ref[...])
    o_ref[...] = _layer_norm(r, g_ref[...], b_ref[...])


def _ffn_ln(x, wg, wu, wo, g, b):
    m = x.shape[0]
    tm = min(ROW_TILE, m)
    row = pl.BlockSpec((tm, D_MODEL), lambda i: (i, 0))
    return pl.pallas_call(
        _ffn_ln_kernel,
        grid=(m // tm,),
        in_specs=[row, _const_spec(wg.shape), _const_spec(wu.shape), _const_spec(wo.shape),
                  _const_spec(g.shape), _const_spec(b.shape)],
        out_specs=row,
        out_shape=jax.ShapeDtypeStruct((m, D_MODEL), F32),
        compiler_params=_params("parallel"),
        name="ffn_ln",
    )(x, wg, wu, wo, g, b)


_PROJ_GROUPS_A = (("gq", GLA_QK), ("gk", GLA_QK), ("gv", GLA_V), ("gr", GLA_V))
_PROJ_GROUPS_B = (("sq", SB_W), ("sk", SB_W), ("sv", SB_W), ("ga", D_MODEL), ("gb", D_MODEL))


def _proj_kernel(h_ref, wa_ref, wb_ref, wlr_ref, wup_ref, bup_ref,
                 gq_ref, gk_ref, gv_ref, gr_ref, la_ref, sq_ref, sk_ref, sv_ref, ga_ref, gb_ref,
                 skb_ref, svb_ref):
    hb = h_ref[...].astype(BF16)
    outs = dict(gq=gq_ref, gk=gk_ref, gv=gv_ref, gr=gr_ref, sq=sq_ref, sk=sk_ref, sv=sv_ref,
                ga=ga_ref, gb=gb_ref)
    for w_ref, groups in ((wa_ref, _PROJ_GROUPS_A), (wb_ref, _PROJ_GROUPS_B)):
        lo = 0
        for name, width in groups:
            y = _dot(hb, w_ref[:, lo:lo + width])
            lo += width
            if name == "gq":
                y = y * (GLA_DK ** -0.5)
            if name in ("sq", "gv"):
                outs[name][...] = y.astype(BF16)
            else:
                outs[name][...] = y
            if name == "sk":
                skb_ref[...] = y.astype(BF16)
            if name == "sv":
                svb_ref[...] = y.astype(BF16)
    lr = _dot(hb, wlr_ref[...]).astype(BF16)
    pre = _dot(lr, wup_ref[...]) + bup_ref[...]
    la_ref[...] = (jnp.minimum(pre, 0.0) - _softplus_neg_abs(pre)) * (1.0 / GLA_GATE_TEMP)


def _proj(h, w_a, w_b, w_lr, w_up, b_up):
    m = h.shape[0]
    tm = min(PROJ_ROW_TILE, m)

    def row(width):
        return pl.BlockSpec((tm, width), lambda i: (i, 0))

    widths = (GLA_QK, GLA_QK, GLA_V, GLA_V, GLA_QK, SB_W, SB_W, SB_W, D_MODEL, D_MODEL, SB_W, SB_W)
    dtypes = (F32, F32, BF16, F32, F32, BF16, F32, F32, F32, F32, BF16, BF16)
    consts = (w_a, w_b, w_lr, w_up, b_up)
    return pl.pallas_call(
        _proj_kernel,
        grid=(m // tm,),
        in_specs=[row(D_MODEL)] + [_const_spec(c.shape) for c in consts],
        out_specs=[row(w) for w in widths],
        out_shape=[jax.ShapeDtypeStruct((m, w), d) for w, d in zip(widths, dtypes)],
        compiler_params=_params("parallel"),
        name="proj",
    )(h, *consts)


def _gla_cumsum_matrix(tb):
    r = lax.broadcasted_iota(jnp.int32, (2 * tb, tb), 0)
    c = lax.broadcasted_iota(jnp.int32, (2 * tb, tb), 1)
    same_chunk = (r % tb) // CHUNK == c // CHUNK
    return (same_chunk & ((c <= r) | (r >= tb))).astype(BF16)


def _gla_kernel(q_ref, k_ref, v_ref, la_ref, s0_ref, cm_ref, o_ref, sout_ref, s_scr):
    t = pl.program_id(1)

    @pl.when(t == 0)
    def _():
        s_scr[...] = s0_ref[...]

    nb, tb, _ = q_ref.shape
    cm = cm_ref[...]
    rows = lax.broadcasted_iota(jnp.int32, (tb, tb), 0)
    cols = lax.broadcasted_iota(jnp.int32, (tb, tb), 1)
    shift = CHUNK.bit_length() - 1
    causal = (cols <= rows) & ((rows >> shift) == (cols >> shift))

    for bi in range(nb):
        for h in range(GLA_HEADS):
            ks = slice(h * GLA_DK, (h + 1) * GLA_DK)
            vs = slice(h * GLA_DV, (h + 1) * GLA_DV)
            cs = _dot(cm, jnp.concatenate(_split3(la_ref[bi, :, ks]), axis=1))
            cs = cs[:, :GLA_DK] + cs[:, GLA_DK:2 * GLA_DK] + cs[:, 2 * GLA_DK:]
            b, b_tot = cs[:tb], cs[tb:]
            q = q_ref[bi, :, ks]
            k = k_ref[bi, :, ks]
            v = v_ref[bi, :, vs]
            qg = (q * jnp.exp(b)).astype(BF16)
            kg = (k * jnp.exp(-b)).astype(BF16)
            kd = (k * jnp.exp(b_tot - b)).astype(BF16)
            att = jnp.where(causal, _dot_nt(qg, kg), 0.0).astype(BF16)
            o_intra = _dot(att, v)
            s = s_scr[bi, h]
            for c in range(tb // CHUNK):
                rs = slice(c * CHUNK, (c + 1) * CHUNK)
                o_ref[bi, rs, vs] = o_intra[rs] + _dot(qg[rs], s.astype(BF16))
                decay = jnp.exp(jnp.broadcast_to(b_tot[c * CHUNK:c * CHUNK + 1], (GLA_DK, GLA_DK))).T
                s = jnp.concatenate([decay, decay], axis=1) * s + _dot_tn(kd[rs], v[rs])
            s_scr[bi, h] = s

    @pl.when(t == pl.num_programs(1) - 1)
    def _():
        sout_ref[...] = s_scr[...]


def _gla(q, k, v, la, s0):
    bsz, t, _ = q.shape
    tb = min(GLA_ROWS, t)
    nb = max(1, min(bsz, GLA_ROWS // t))
    assert t % tb == 0 and bsz % nb == 0 and tb % CHUNK == 0
    cm = _gla_cumsum_matrix(tb)
    qk_spec = pl.BlockSpec((nb, tb, GLA_QK), lambda b, i: (b, i, 0))
    v_spec = pl.BlockSpec((nb, tb, GLA_V), lambda b, i: (b, i, 0))
    s_spec = pl.BlockSpec((nb, GLA_HEADS, GLA_DK, GLA_DV), lambda b, i: (b, 0, 0, 0))
    return pl.pallas_call(
        _gla_kernel,
        grid=(bsz // nb, t // tb),
        in_specs=[qk_spec, qk_spec, v_spec, qk_spec, s_spec, _const_spec(cm.shape)],
        out_specs=[v_spec, s_spec],
        out_shape=[jax.ShapeDtypeStruct((bsz, t, GLA_V), F32),
                   jax.ShapeDtypeStruct((bsz, GLA_HEADS, GLA_DK, GLA_DV), F32)],
        scratch_shapes=[pltpu.VMEM((nb, GLA_HEADS, GLA_DK, GLA_DV), F32)],
        compiler_params=_params("parallel", "arbitrary"),
        name="gla",
    )(q, k, v, la, s0, cm)


def _suffix_matrix(tk):
    r = lax.broadcasted_iota(jnp.int32, (2 * tk, tk), 0) % tk
    c = lax.broadcasted_iota(jnp.int32, (2 * tk, tk), 1)
    return (r >= c).astype(BF16)


def _sb_update(qs, ks, vs, carry_ref, acc_ref, suffix, visible, first):
    heads = range(len(qs))
    sub = suffix.shape[1]
    n_sub = ks[0].shape[0] // sub
    z = [_dot_nt(qs[hd], ks[hd]) * (SB_DH ** -0.5) for hd in heads]
    carry = [None if first else carry_ref[hd] for hd in heads]
    ws = [[None] * n_sub for _ in heads]
    for s in reversed(range(n_sub)):
        vis = None if visible is None else visible[s]
        zs = [z[hd][:, s * sub:(s + 1) * sub] for hd in heads]
        cs = [jnp.maximum(zs[hd], 0.0) + _softplus_neg_abs(zs[hd]) for hd in heads]
        if vis is not None:
            cs = [jnp.where(vis, c, 0.0) for c in cs]
        his = [c.astype(BF16) for c in cs]
        los = [(c - hi.astype(F32)).astype(BF16) for c, hi in zip(cs, his)]
        sums = [_dot(jnp.concatenate([hi, lo], axis=1), suffix) for hi, lo in zip(his, los)]
        for hd in heads:
            e = zs[hd] - sums[hd]
            if carry[hd] is not None:
                e = e - jnp.concatenate([carry[hd]] * (sub // LANES), axis=1)
            w = jnp.exp(e)
            if vis is not None:
                w = jnp.where(vis, w, 0.0)
            ws[hd][s] = w.astype(BF16)
            total = jnp.broadcast_to(jnp.sum(cs[hd], axis=-1, keepdims=True), (cs[hd].shape[0], LANES))
            carry[hd] = total if carry[hd] is None else carry[hd] + total
    for hd in heads:
        pv = _dot(ws[hd][0] if n_sub == 1 else jnp.concatenate(ws[hd], axis=1), vs[hd])
        acc_ref[hd] = pv if first else acc_ref[hd] + pv
        carry_ref[hd] = carry[hd]


def _head(hd):
    return slice(hd * SB_DH, (hd + 1) * SB_DH)


def _sb_prompt_kernel(q_ref, k_ref, v_ref, sfx_ref, o_ref, carry_scr, acc_scr):
    i = pl.program_id(2)
    blk = SB_BLOCK
    heads = q_ref.shape[2] // SB_DH
    sfx = sfx_ref[...]
    rows = lax.broadcasted_iota(jnp.int32, (blk, blk), 0)
    cols = lax.broadcasted_iota(jnp.int32, (blk, blk), 1)
    strictly_earlier = cols < rows
    qs = [q_ref[0, :, _head(hd)] for hd in range(heads)]

    @pl.when(i == 0)
    def _():
        _sb_update(qs, [k_ref[0, pl.ds(0, blk), _head(hd)] for hd in range(heads)],
                   [v_ref[0, pl.ds(0, blk), _head(hd)] for hd in range(heads)],
                   carry_scr, acc_scr, sfx, (strictly_earlier,), True)

    @pl.when(i > 0)
    def _():
        p0 = pl.multiple_of((i - 1) * blk, blk)
        _sb_update(qs, [k_ref[0, pl.ds(p0, 2 * blk), _head(hd)] for hd in range(heads)],
                   [v_ref[0, pl.ds(p0, 2 * blk), _head(hd)] for hd in range(heads)],
                   carry_scr, acc_scr, sfx, (None, strictly_earlier), True)

    n_older = jnp.maximum(i - 1, 0)

    def cond(state):
        j, cmin = state
        return (j < n_older) & (cmin <= SB_DEAD)

    def body(state):
        j, _ = state
        k0 = pl.multiple_of((i - 2 - j) * blk, blk)
        _sb_update(qs, [k_ref[0, pl.ds(k0, blk), _head(hd)] for hd in range(heads)],
                   [v_ref[0, pl.ds(k0, blk), _head(hd)] for hd in range(heads)],
                   carry_scr, acc_scr, sfx, None, False)
        return j + 1, jnp.min(carry_scr[...])

    lax.while_loop(cond, body, (jnp.int32(0), jnp.min(carry_scr[...])))
    for hd in range(heads):
        o_ref[0, :, _head(hd)] = acc_scr[hd].astype(o_ref.dtype)


def _sb_prompt(sq, sk, sv):
    bsz, t, _ = sq.shape
    blk = SB_BLOCK
    assert t % blk == 0
    width = SB_GROUP * SB_DH
    sfx = _suffix_matrix(blk)
    q_spec = pl.BlockSpec((1, blk, width), lambda b, g, i: (b, i, g))
    kv_spec = pl.BlockSpec((1, t, width), lambda b, g, i: (b, 0, g), pipeline_mode=pl.Buffered(1))
    return pl.pallas_call(
        _sb_prompt_kernel,
        grid=(bsz, SB_HEADS // SB_GROUP, t // blk),
        in_specs=[q_spec, kv_spec, kv_spec, _const_spec(sfx.shape)],
        out_specs=q_spec,
        out_shape=jax.ShapeDtypeStruct((bsz, t, SB_W), BF16),
        scratch_shapes=[pltpu.VMEM((SB_GROUP, blk, LANES), F32), pltpu.VMEM((SB_GROUP, blk, SB_DH), F32)],
        compiler_params=_params("parallel", "parallel", "arbitrary"),
        name="sb_prompt",
    )(sq, sk, sv, sfx)


def _sb_sample_kernel(q_ref, kn_ref, vn_ref, kw_ref, vw_ref, k_hbm, v_hbm, sfx_ref, o_ref,
                      carry_scr, acc_scr, kbuf, vbuf, sem, *, n_past):
    b = pl.program_id(0)
    tq = q_ref.shape[1]
    blk = SB_BLOCK
    win = blk - tq
    sfx = sfx_ref[...]
    rows = lax.broadcasted_iota(jnp.int32, (tq, blk), 0)
    cols = lax.broadcasted_iota(jnp.int32, (tq, blk), 1)

    def head_rows(ref, hd, first, n):
        return ref[pl.ds(first * SB_HEADS + hd, n, stride=SB_HEADS), :].astype(BF16)

    visible = (cols < win) | (cols - win < rows)
    qs = [q_ref[0, :, _head(hd)] for hd in range(SB_HEADS)]
    _sb_update(qs,
               [jnp.concatenate([head_rows(kw_ref.at[0], hd, blk - win, win), kn_ref[0, :, _head(hd)]], axis=0)
                for hd in range(SB_HEADS)],
               [jnp.concatenate([head_rows(vw_ref.at[0], hd, blk - win, win), vn_ref[0, :, _head(hd)]], axis=0)
                for hd in range(SB_HEADS)],
               carry_scr, acc_scr, sfx, (visible,), True)

    n_left = n_past - win
    n_older = -(-n_left // blk)

    def cond(state):
        j, cmin = state
        return (j < n_older) & (cmin <= SB_DEAD)

    def body(state):
        j, _ = state
        end = n_left - j * blk
        start = jnp.maximum(end - blk, 0)
        row0 = pl.multiple_of(start * SB_HEADS, SB_HEADS)
        copy_k = pltpu.make_async_copy(k_hbm.at[b, pl.ds(row0, blk * SB_HEADS)], kbuf, sem.at[0])
        copy_v = pltpu.make_async_copy(v_hbm.at[b, pl.ds(row0, blk * SB_HEADS)], vbuf, sem.at[1])
        copy_k.start()
        copy_v.start()
        copy_k.wait()
        copy_v.wait()
        unread = cols < end - start
        _sb_update(qs, [head_rows(kbuf, hd, 0, blk) for hd in range(SB_HEADS)],
                   [head_rows(vbuf, hd, 0, blk) for hd in range(SB_HEADS)],
                   carry_scr, acc_scr, sfx, (unread,), False)
        return j + 1, jnp.min(carry_scr[...])

    lax.while_loop(cond, body, (jnp.int32(0), jnp.min(carry_scr[...])))
    for hd in range(SB_HEADS):
        o_ref[0, :, _head(hd)] = acc_scr[hd].astype(o_ref.dtype)


def _sb_sample(sq, sk, sv, k_past, v_past):
    bsz, t, _ = sq.shape
    p = k_past.shape[1]
    blk = SB_BLOCK
    assert p % blk == 0 and p >= blk and t < blk and t % 16 == 0
    k_past = k_past.reshape(bsz, p * SB_HEADS, SB_DH)
    v_past = v_past.reshape(bsz, p * SB_HEADS, SB_DH)
    sfx = _suffix_matrix(blk)
    new_spec = pl.BlockSpec((1, t, SB_W), lambda b: (b, 0, 0))
    window_spec = pl.BlockSpec((1, blk * SB_HEADS, SB_DH), lambda b: (b, p // blk - 1, 0))
    hbm_spec = pl.BlockSpec(memory_space=pl.ANY)
    return pl.pallas_call(
        functools.partial(_sb_sample_kernel, n_past=p),
        grid=(bsz,),
        in_specs=[new_spec, new_spec, new_spec, window_spec, window_spec, hbm_spec, hbm_spec,
                  _const_spec(sfx.shape)],
        out_specs=new_spec,
        out_shape=jax.ShapeDtypeStruct((bsz, t, SB_W), BF16),
        scratch_shapes=[pltpu.VMEM((SB_HEADS, t, LANES), F32), pltpu.VMEM((SB_HEADS, t, SB_DH), F32),
                        pltpu.VMEM((blk * SB_HEADS, SB_DH), F32), pltpu.VMEM((blk * SB_HEADS, SB_DH), F32),
                        pltpu.SemaphoreType.DMA((2,))],
        compiler_params=_params("arbitrary"),
        name="sb_sample",
    )(sq, sk, sv, k_past, v_past, k_past, v_past, sfx)


def _merge_ln_kernel(h_ref, oa_ref, gr_ref, ob_ref, ga_ref, gb_ref, gn_ref,
                     wa_ref, wb_ref, wo_ref, g_ref, b_ref, o_ref):
    o = oa_ref[...]
    gn = gn_ref[...]
    parts = []
    for hd in range(GLA_HEADS):
        vs = slice(hd * GLA_DV, (hd + 1) * GLA_DV)
        oh = o[:, vs]
        ms = jnp.mean(oh * oh, axis=-1, keepdims=True)
        parts.append(oh * lax.rsqrt(ms + LN_EPS) * gn[:, vs])
    gr = gr_ref[...]
    oa = jnp.concatenate(parts, axis=1) * (gr * jax.nn.sigmoid(gr))
    branch_a = _dot(oa.astype(BF16), wa_ref[...])
    branch_b = _dot(ob_ref[...], wb_ref[...])
    merged = jax.nn.sigmoid(ga_ref[...]) * branch_a + jax.nn.sigmoid(gb_ref[...]) * branch_b
    mix = _dot(merged.astype(BF16), wo_ref[...])
    o_ref[...] = _layer_norm(DN_ALPHA * h_ref[...] + mix, g_ref[...], b_ref[...])


def _merge_ln(h, oa, gr, ob, ga, gb, gn, wa, wb, wo, g, b):
    m = h.shape[0]
    tm = min(ROW_TILE, m)
    row = pl.BlockSpec((tm, D_MODEL), lambda i: (i, 0))
    consts = [gn, wa, wb, wo, g, b]
    return pl.pallas_call(
        _merge_ln_kernel,
        grid=(m // tm,),
        in_specs=[row] * 6 + [_const_spec(c.shape) for c in consts],
        out_specs=row,
        out_shape=jax.ShapeDtypeStruct((m, D_MODEL), F32),
        compiler_params=_params("parallel"),
        name="merge_ln",
    )(h, oa, gr, ob, ga, gb, *consts)


def _prep_ffn(w_in, w_out):
    return w_in[:, :D_FF].astype(BF16), w_in[:, D_FF:].astype(BF16), w_out.astype(BF16)


def _row(v):
    return v.reshape(1, -1)


def _layer(x, s0, k_past, v_past, wts):
    bsz, t, _ = x.shape
    m = bsz * t
    h = _ffn_ln(x.reshape(m, D_MODEL), *wts["ffn1"])
    gq, gk, gv, gr, la, sq, sk, sv, ga, gb, skb, svb = _proj(h, *wts["proj"])

    def seq(a):
        return a.reshape(bsz, t, a.shape[-1])

    o_a, s_new = _gla(seq(gq), seq(gk), seq(gv), seq(la), s0)
    if k_past is None:
        o_b = _sb_prompt(seq(sq), seq(skb), seq(svb))
    else:
        o_b = _sb_sample(seq(sq), seq(skb), seq(svb), k_past, v_past)
    h2 = _merge_ln(h, o_a.reshape(m, GLA_V), gr, o_b.reshape(m, SB_W), ga, gb, *wts["merge"])
    y = _ffn_ln(h2, *wts["ffn2"])
    return (y.reshape(bsz, t, D_MODEL), s_new,
            sk.reshape(bsz, t, SB_HEADS, SB_DH), sv.reshape(bsz, t, SB_HEADS, SB_DH))


def kernel(x_prompt, x_sample, state_gla, cache_sb_k, cache_sb_v, ffn1_w_in, ffn1_w_out, ln1_g, ln1_b, w_in, w_gla_gate_up, b_gla_gate, g_gla_norm, w_gla_o, w_sb_o, w_out, ln2_g, ln2_b, ffn2_w_in, ffn2_w_out, ln3_g, ln3_b):
    xp, xs = x_prompt, x_sample
    bp = xp.shape[0]
    outs = [[] for _ in range(6)]
    lr_lo = 2 * GLA_QK + 2 * GLA_V
    lr_hi = lr_lo + GLA_GATE_RANK
    for l in range(DEPTH):
        w_lr = jnp.pad(w_in[l][:, lr_lo:lr_hi], ((0, 0), (0, LANES - GLA_GATE_RANK))).astype(BF16)
        w_up = jnp.pad(w_gla_gate_up[l], ((0, LANES - GLA_GATE_RANK), (0, 0))).astype(BF16)
        wts = {
            "ffn1": (*_prep_ffn(ffn1_w_in[l], ffn1_w_out[l]), _row(ln1_g[l]), _row(ln1_b[l])),
            "proj": (w_in[l][:, :lr_lo].astype(BF16), w_in[l][:, lr_hi:].astype(BF16), w_lr, w_up,
                     _row(b_gla_gate[l])),
            "merge": (_row(g_gla_norm[l]), w_gla_o[l].astype(BF16), w_sb_o[l].astype(BF16),
                      w_out[l].astype(BF16), _row(ln2_g[l]), _row(ln2_b[l])),
            "ffn2": (*_prep_ffn(ffn2_w_in[l], ffn2_w_out[l]), _row(ln3_g[l]), _row(ln3_b[l])),
        }
        s0_p = jnp.zeros((bp, GLA_HEADS, GLA_DK, GLA_DV), F32)
        xp, sp, kp, vp = _layer(xp, s0_p, None, None, wts)
        xs, ss, ksn, vsn = _layer(xs, state_gla[l], cache_sb_k[l], cache_sb_v[l], wts)
        for lst, val in zip(outs, (sp, kp, vp, ss, ksn, vsn)):
            lst.append(val)
    return (xp, xs, *(jnp.stack(o) for o in outs))
```

```python
import functools

import jax
import jax.numpy as jnp
from jax import lax
from jax.experimental import pallas as pl
from jax.experimental.pallas import tpu as pltpu

F32 = jnp.float32
BF16 = jnp.bfloat16

D_MODEL = 1024
CHUNK = 64
GLA_HEADS = 4
GLA_DK = 128
GLA_DV = 256
GLA_QK = GLA_HEADS * GLA_DK
GLA_V = GLA_HEADS * GLA_DV
GLA_GATE_RANK = 16
GLA_GATE_TEMP = 16.0
SB_HEADS = 8
SB_DH = 128
SB_W = SB_HEADS * SB_DH
D_FF = 2816
LN_EPS = 1e-5
DEPTH = 1
DN_ALPHA = (2 * DEPTH) ** 0.25

LANES = 128
GLA_ROWS = 512
GLA_BATCHED_ROWS = 256
ROW_TILE = 512
PROJ_ROW_TILE = 256
TAIL_ROW_TILE = 256
SB_BLOCK = 256
SB_GROUP = 8
SB_DEAD = 110.0
VMEM_LIMIT = 56 * 1024 * 1024


def _layer_norm(r, g, b):
    mu = jnp.mean(r, axis=-1, keepdims=True)
    d = r - mu
    var = jnp.mean(d * d, axis=-1, keepdims=True)
    return d * lax.rsqrt(var + LN_EPS) * g + b


def _softplus_neg_abs(z):
    return jnp.log(1.0 + jnp.exp(-jnp.abs(z)))


def _split3(x):
    x1 = x.astype(BF16)
    r1 = x - x1.astype(F32)
    x2 = r1.astype(BF16)
    r2 = r1 - x2.astype(F32)
    return x1, x2, r2.astype(BF16)


def _dot(a, b):
    return jnp.dot(a, b, preferred_element_type=F32)


def _dot_nt(a, b):
    return lax.dot_general(a, b, (((1,), (1,)), ((), ())), preferred_element_type=F32)


def _dot_tn(a, b):
    return lax.dot_general(a, b, (((0,), (0,)), ((), ())), preferred_element_type=F32)


def _const_spec(shape):
    nd = len(shape)
    return pl.BlockSpec(shape, lambda *_: (0,) * nd, pipeline_mode=pl.Buffered(1))


def _params(*sem):
    return pltpu.CompilerParams(dimension_semantics=sem, vmem_limit_bytes=VMEM_LIMIT)


def _swiglu_ln(x, wg_ref, wu_ref, wo_ref, g_ref, b_ref):
    xb = x.astype(BF16)
    gate = _dot(xb, wg_ref[...])
    up = _dot(xb, wu_ref[...])
    a = (gate * jax.nn.sigmoid(gate) * up).astype(BF16)
    return _layer_norm(DN_ALPHA * x + 0.5 * _dot(a, wo_ref[...]), g_ref[...], b_ref[...])


def _ffn_ln_kernel(x_ref, wg_ref, wu_ref, wo_ref, g_ref, b_ref, o_ref):
    o_ref[...] = _swiglu_ln(x_ref[...], wg_ref, wu_ref, wo_ref, g_ref, b_ref)


def _ffn_ln(x, wg, wu, wo, g, b):
    m = x.shape[0]
    tm = min(ROW_TILE, m)
    row = pl.BlockSpec((tm, D_MODEL), lambda i: (i, 0))
    return pl.pallas_call(
        _ffn_ln_kernel,
        grid=(m // tm,),
        in_specs=[row, _const_spec(wg.shape), _const_spec(wu.shape), _const_spec(wo.shape),
                  _const_spec(g.shape), _const_spec(b.shape)],
        out_specs=row,
        out_shape=jax.ShapeDtypeStruct((m, D_MODEL), F32),
        compiler_params=_params("parallel"),
        name="ffn_ln",
    )(x, wg, wu, wo, g, b)


_PROJ_GROUPS_A = (("gq", GLA_QK), ("gk", GLA_QK), ("gv", GLA_V), ("gr", GLA_V))
_PROJ_GROUPS_B = (("sq", SB_W), ("sk", SB_W), ("sv", SB_W), ("ga", D_MODEL), ("gb", D_MODEL))


def _proj_kernel(h_ref, wa_ref, wb_ref, wlr_ref, wup_ref, bup_ref,
                 gq_ref, gk_ref, gv_ref, gr_ref, la_ref, sq_ref, sk_ref, sv_ref, ga_ref, gb_ref,
                 skb_ref, svb_ref):
    hb = h_ref[...].astype(BF16)
    outs = dict(gq=gq_ref, gk=gk_ref, gv=gv_ref, gr=gr_ref, sq=sq_ref, sk=sk_ref, sv=sv_ref,
                ga=ga_ref, gb=gb_ref)
    for w_ref, groups in ((wa_ref, _PROJ_GROUPS_A), (wb_ref, _PROJ_GROUPS_B)):
        lo = 0
        for name, width in groups:
            y = _dot(hb, w_ref[:, lo:lo + width])
            lo += width
            if name == "gq":
                y = y * (GLA_DK ** -0.5)
            if name in ("sq", "gv"):
                outs[name][...] = y.astype(BF16)
            else:
                outs[name][...] = y
            if name == "sk":
                skb_ref[...] = y.astype(BF16)
            if name == "sv":
                svb_ref[...] = y.astype(BF16)
    lr = _dot(hb, wlr_ref[...]).astype(BF16)
    pre = _dot(lr, wup_ref[...]) + bup_ref[...]
    la_ref[...] = (jnp.minimum(pre, 0.0) - _softplus_neg_abs(pre)) * (1.0 / GLA_GATE_TEMP)


def _proj(h, w_a, w_b, w_lr, w_up, b_up):
    m = h.shape[0]
    tm = min(PROJ_ROW_TILE, m)

    def row(width):
        return pl.BlockSpec((tm, width), lambda i: (i, 0))

    widths = (GLA_QK, GLA_QK, GLA_V, GLA_V, GLA_QK, SB_W, SB_W, SB_W, D_MODEL, D_MODEL, SB_W, SB_W)
    dtypes = (F32, F32, BF16, F32, F32, BF16, F32, F32, F32, F32, BF16, BF16)
    consts = (w_a, w_b, w_lr, w_up, b_up)
    return pl.pallas_call(
        _proj_kernel,
        grid=(m // tm,),
        in_specs=[row(D_MODEL)] + [_const_spec(c.shape) for c in consts],
        out_specs=[row(w) for w in widths],
        out_shape=[jax.ShapeDtypeStruct((m, w), d) for w, d in zip(widths, dtypes)],
        compiler_params=_params("parallel"),
        name="proj",
    )(h, *consts)


def _gla_cumsum_matrix(tb):
    r = lax.broadcasted_iota(jnp.int32, (2 * tb, tb), 0)
    c = lax.broadcasted_iota(jnp.int32, (2 * tb, tb), 1)
    same_chunk = (r % tb) // CHUNK == c // CHUNK
    return (same_chunk & ((c <= r) | (r >= tb))).astype(BF16)


def _gla_kernel(q_ref, k_ref, v_ref, la_ref, gr_ref, gn_ref, s0_ref, cm_ref, o_ref, sout_ref, s_scr):
    t = pl.program_id(1)

    @pl.when(t == 0)
    def _():
        s_scr[...] = s0_ref[...]

    nb, tb, _ = q_ref.shape
    sb = cm_ref.shape[1]
    cm = cm_ref[...]
    gn = gn_ref[...]
    rows = lax.broadcasted_iota(jnp.int32, (sb, sb), 0)
    cols = lax.broadcasted_iota(jnp.int32, (sb, sb), 1)
    shift = CHUNK.bit_length() - 1
    causal = (cols <= rows) & ((rows >> shift) == (cols >> shift))

    for bi in range(nb):
        for r0 in range(0, tb, sb):
            rb = slice(r0, r0 + sb)
            for h in range(GLA_HEADS):
                ks = slice(h * GLA_DK, (h + 1) * GLA_DK)
                vs = slice(h * GLA_DV, (h + 1) * GLA_DV)
                cs = _dot(cm, jnp.concatenate(_split3(la_ref[bi, rb, ks]), axis=1))
                cs = cs[:, :GLA_DK] + cs[:, GLA_DK:2 * GLA_DK] + cs[:, 2 * GLA_DK:]
                b, b_tot = cs[:sb], cs[sb:]
                q = q_ref[bi, rb, ks]
                k = k_ref[bi, rb, ks]
                v = v_ref[bi, rb, vs]
                qg = (q * jnp.exp(b)).astype(BF16)
                kg = (k * jnp.exp(-b)).astype(BF16)
                kd = (k * jnp.exp(b_tot - b)).astype(BF16)
                att = jnp.where(causal, _dot_nt(qg, kg), 0.0).astype(BF16)
                o_intra = _dot(att, v)
                s = s_scr[bi, h]
                for c in range(sb // CHUNK):
                    rs = slice(c * CHUNK, (c + 1) * CHUNK)
                    ro = slice(r0 + c * CHUNK, r0 + (c + 1) * CHUNK)
                    o = o_intra[rs] + _dot(qg[rs], s.astype(BF16))
                    o = o * lax.rsqrt(jnp.mean(o * o, axis=-1, keepdims=True) + LN_EPS) * gn[:, vs]
                    gate = gr_ref[bi, ro, vs]
                    o_ref[bi, ro, vs] = (o * (gate * jax.nn.sigmoid(gate))).astype(BF16)
                    decay = jnp.exp(jnp.broadcast_to(b_tot[c * CHUNK:c * CHUNK + 1], (GLA_DK, GLA_DK))).T
                    s = jnp.concatenate([decay, decay], axis=1) * s + _dot_tn(kd[rs], v[rs])
                s_scr[bi, h] = s

    @pl.when(t == pl.num_programs(1) - 1)
    def _():
        sout_ref[...] = s_scr[...]


def _gla(q, k, v, la, gr, gn, s0):
    bsz, t, _ = q.shape
    tb = min(GLA_ROWS, t)
    nb = max(1, min(bsz, GLA_ROWS // t))
    assert t % tb == 0 and bsz % nb == 0 and tb % CHUNK == 0
    cm = _gla_cumsum_matrix(min(tb, GLA_BATCHED_ROWS))
    qk_spec = pl.BlockSpec((nb, tb, GLA_QK), lambda b, i: (b, i, 0))
    v_spec = pl.BlockSpec((nb, tb, GLA_V), lambda b, i: (b, i, 0))
    s_spec = pl.BlockSpec((nb, GLA_HEADS, GLA_DK, GLA_DV), lambda b, i: (b, 0, 0, 0))
    return pl.pallas_call(
        _gla_kernel,
        grid=(bsz // nb, t // tb),
        in_specs=[qk_spec, qk_spec, v_spec, qk_spec, v_spec, _const_spec(gn.shape), s_spec,
                  _const_spec(cm.shape)],
        out_specs=[v_spec, s_spec],
        out_shape=[jax.ShapeDtypeStruct((bsz, t, GLA_V), BF16),
                   jax.ShapeDtypeStruct((bsz, GLA_HEADS, GLA_DK, GLA_DV), F32)],
        scratch_shapes=[pltpu.VMEM((nb, GLA_HEADS, GLA_DK, GLA_DV), F32)],
        compiler_params=_params("parallel", "arbitrary"),
        name="gla",
    )(q, k, v, la, gr, gn, s0, cm)


def _suffix_matrix(tk):
    r = lax.broadcasted_iota(jnp.int32, (2 * tk, tk), 0) % tk
    c = lax.broadcasted_iota(jnp.int32, (2 * tk, tk), 1)
    return (r >= c).astype(BF16)


def _sb_update(qs, ks, vs, carry_ref, acc_ref, suffix, visible, first):
    heads = range(len(qs))
    sub = suffix.shape[1]
    n_sub = ks[0].shape[0] // sub
    z = [_dot_nt(qs[hd], ks[hd]) * (SB_DH ** -0.5) for hd in heads]
    carry = [None if first else carry_ref[hd] for hd in heads]
    ws = [[None] * n_sub for _ in heads]
    for s in reversed(range(n_sub)):
        vis = None if visible is None else visible[s]
        zs = [z[hd][:, s * sub:(s + 1) * sub] for hd in heads]
        cs = [jnp.maximum(zs[hd], 0.0) + _softplus_neg_abs(zs[hd]) for hd in heads]
        if vis is not None:
            cs = [jnp.where(vis, c, 0.0) for c in cs]
        his = [c.astype(BF16) for c in cs]
        los = [(c - hi.astype(F32)).astype(BF16) for c, hi in zip(cs, his)]
        sums = [_dot(jnp.concatenate([hi, lo], axis=1), suffix) for hi, lo in zip(his, los)]
        for hd in heads:
            e = zs[hd] - sums[hd]
            if carry[hd] is not None:
                e = e - jnp.concatenate([carry[hd]] * (sub // LANES), axis=1)
            w = jnp.exp(e)
            if vis is not None:
                w = jnp.where(vis, w, 0.0)
            ws[hd][s] = w.astype(BF16)
            total = jnp.broadcast_to(jnp.sum(cs[hd], axis=-1, keepdims=True), (cs[hd].shape[0], LANES))
            carry[hd] = total if carry[hd] is None else carry[hd] + total
    for hd in heads:
        pv = _dot(ws[hd][0] if n_sub == 1 else jnp.concatenate(ws[hd], axis=1), vs[hd])
        acc_ref[hd] = pv if first else acc_ref[hd] + pv
        carry_ref[hd] = carry[hd]


def _head(hd):
    return slice(hd * SB_DH, (hd + 1) * SB_DH)


def _sb_prompt_kernel(q_ref, k_ref, v_ref, sfx_ref, o_ref, carry_scr, acc_scr):
    i = pl.program_id(2)
    blk = SB_BLOCK
    heads = q_ref.shape[2] // SB_DH
    sfx = sfx_ref[...]
    rows = lax.broadcasted_iota(jnp.int32, (blk, blk), 0)
    cols = lax.broadcasted_iota(jnp.int32, (blk, blk), 1)
    strictly_earlier = cols < rows
    qs = [q_ref[0, :, _head(hd)] for hd in range(heads)]

    @pl.when(i == 0)
    def _():
        _sb_update(qs, [k_ref[0, pl.ds(0, blk), _head(hd)] for hd in range(heads)],
                   [v_ref[0, pl.ds(0, blk), _head(hd)] for hd in range(heads)],
                   carry_scr, acc_scr, sfx, (strictly_earlier,), True)

    @pl.when(i > 0)
    def _():
        p0 = pl.multiple_of((i - 1) * blk, blk)
        _sb_update(qs, [k_ref[0, pl.ds(p0, 2 * blk), _head(hd)] for hd in range(heads)],
                   [v_ref[0, pl.ds(p0, 2 * blk), _head(hd)] for hd in range(heads)],
                   carry_scr, acc_scr, sfx, (None, strictly_earlier), True)

    n_older = jnp.maximum(i - 1, 0)

    def cond(state):
        j, cmin = state
        return (j < n_older) & (cmin <= SB_DEAD)

    def body(state):
        j, _ = state
        k0 = pl.multiple_of((i - 2 - j) * blk, blk)
        _sb_update(qs, [k_ref[0, pl.ds(k0, blk), _head(hd)] for hd in range(heads)],
                   [v_ref[0, pl.ds(k0, blk), _head(hd)] for hd in range(heads)],
                   carry_scr, acc_scr, sfx, None, False)
        return j + 1, jnp.min(carry_scr[...])

    lax.while_loop(cond, body, (jnp.int32(0), jnp.min(carry_scr[...])))
    for hd in range(heads):
        o_ref[0, :, _head(hd)] = acc_scr[hd].astype(o_ref.dtype)


def _sb_prompt(sq, sk, sv):
    bsz, t, _ = sq.shape
    blk = SB_BLOCK
    assert t % blk == 0
    width = SB_GROUP * SB_DH
    sfx = _suffix_matrix(blk)
    q_spec = pl.BlockSpec((1, blk, width), lambda b, g, i: (b, i, g))
    kv_spec = pl.BlockSpec((1, t, width), lambda b, g, i: (b, 0, g), pipeline_mode=pl.Buffered(1))
    return pl.pallas_call(
        _sb_prompt_kernel,
        grid=(bsz, SB_HEADS // SB_GROUP, t // blk),
        in_specs=[q_spec, kv_spec, kv_spec, _const_spec(sfx.shape)],
        out_specs=q_spec,
        out_shape=jax.ShapeDtypeStruct((bsz, t, SB_W), BF16),
        scratch_shapes=[pltpu.VMEM((SB_GROUP, blk, LANES), F32), pltpu.VMEM((SB_GROUP, blk, SB_DH), F32)],
        compiler_params=_params("parallel", "parallel", "arbitrary"),
        name="sb_prompt",
    )(sq, sk, sv, sfx)


def _sb_sample_kernel(q_ref, kn_ref, vn_ref, kw_ref, vw_ref, k_hbm, v_hbm, sfx_ref, o_ref,
                      carry_scr, acc_scr, kbuf, vbuf, sem, *, n_past):
    b = pl.program_id(0)
    tq = q_ref.shape[1]
    blk = SB_BLOCK
    win = blk - tq
    sfx = sfx_ref[...]
    rows = lax.broadcasted_iota(jnp.int32, (tq, blk), 0)
    cols = lax.broadcasted_iota(jnp.int32, (tq, blk), 1)

    def head_rows(ref, hd, first, n):
        return ref[pl.ds(first * SB_HEADS + hd, n, stride=SB_HEADS), :].astype(BF16)

    visible = (cols < win) | (cols - win < rows)
    qs = [q_ref[0, :, _head(hd)] for hd in range(SB_HEADS)]
    _sb_update(qs,
               [jnp.concatenate([head_rows(kw_ref.at[0], hd, blk - win, win), kn_ref[0, :, _head(hd)]], axis=0)
                for hd in range(SB_HEADS)],
               [jnp.concatenate([head_rows(vw_ref.at[0], hd, blk - win, win), vn_ref[0, :, _head(hd)]], axis=0)
                for hd in range(SB_HEADS)],
               carry_scr, acc_scr, sfx, (visible,), True)

    n_left = n_past - win
    n_older = -(-n_left // blk)

    def cond(state):
        j, cmin = state
        return (j < n_older) & (cmin <= SB_DEAD)

    def body(state):
        j, _ = state
        end = n_left - j * blk
        start = jnp.maximum(end - blk, 0)
        row0 = pl.multiple_of(start * SB_HEADS, SB_HEADS)
        copy_k = pltpu.make_async_copy(k_hbm.at[b, pl.ds(row0, blk * SB_HEADS)], kbuf, sem.at[0])
        copy_v = pltpu.make_async_copy(v_hbm.at[b, pl.ds(row0, blk * SB_HEADS)], vbuf, sem.at[1])
        copy_k.start()
        copy_v.start()
        copy_k.wait()
        copy_v.wait()
        unread = cols < end - start
        _sb_update(qs, [head_rows(kbuf, hd, 0, blk) for hd in range(SB_HEADS)],
                   [head_rows(vbuf, hd, 0, blk) for hd in range(SB_HEADS)],
                   carry_scr, acc_scr, sfx, (unread,), False)
        return j + 1, jnp.min(carry_scr[...])

    lax.while_loop(cond, body, (jnp.int32(0), jnp.min(carry_scr[...])))
    for hd in range(SB_HEADS):
        o_ref[0, :, _head(hd)] = acc_scr[hd].astype(o_ref.dtype)


def _sb_sample(sq, sk, sv, k_past, v_past):
    bsz, t, _ = sq.shape
    p = k_past.shape[1]
    blk = SB_BLOCK
    assert p % blk == 0 and p >= blk and t < blk and t % 16 == 0
    k_past = k_past.reshape(bsz, p * SB_HEADS, SB_DH)
    v_past = v_past.reshape(bsz, p * SB_HEADS, SB_DH)
    sfx = _suffix_matrix(blk)
    new_spec = pl.BlockSpec((1, t, SB_W), lambda b: (b, 0, 0))
    window_spec = pl.BlockSpec((1, blk * SB_HEADS, SB_DH), lambda b: (b, p // blk - 1, 0))
    hbm_spec = pl.BlockSpec(memory_space=pl.ANY)
    return pl.pallas_call(
        functools.partial(_sb_sample_kernel, n_past=p),
        grid=(bsz,),
        in_specs=[new_spec, new_spec, new_spec, window_spec, window_spec, hbm_spec, hbm_spec,
                  _const_spec(sfx.shape)],
        out_specs=new_spec,
        out_shape=jax.ShapeDtypeStruct((bsz, t, SB_W), BF16),
        scratch_shapes=[pltpu.VMEM((SB_HEADS, t, LANES), F32), pltpu.VMEM((SB_HEADS, t, SB_DH), F32),
                        pltpu.VMEM((blk * SB_HEADS, SB_DH), F32), pltpu.VMEM((blk * SB_HEADS, SB_DH), F32),
                        pltpu.SemaphoreType.DMA((2,))],
        compiler_params=_params("arbitrary"),
        name="sb_sample",
    )(sq, sk, sv, k_past, v_past, k_past, v_past, sfx)


def _merge_ffn_kernel(h_ref, oa_ref, ob_ref, ga_ref, gb_ref, wa_ref, wb_ref, wm_ref, g2_ref, b2_ref,
                      wg_ref, wu_ref, wo_ref, g3_ref, b3_ref, o_ref):
    branch_a = _dot(oa_ref[...], wa_ref[...])
    branch_b = _dot(ob_ref[...], wb_ref[...])
    merged = jax.nn.sigmoid(ga_ref[...]) * branch_a + jax.nn.sigmoid(gb_ref[...]) * branch_b
    mix = _dot(merged.astype(BF16), wm_ref[...])
    h2 = _layer_norm(DN_ALPHA * h_ref[...] + mix, g2_ref[...], b2_ref[...])
    o_ref[...] = _swiglu_ln(h2, wg_ref, wu_ref, wo_ref, g3_ref, b3_ref)


def _merge_ffn(h, oa, ob, ga, gb, *consts):
    m = h.shape[0]
    tm = min(TAIL_ROW_TILE, m)
    row = pl.BlockSpec((tm, D_MODEL), lambda i: (i, 0))
    return pl.pallas_call(
        _merge_ffn_kernel,
        grid=(m // tm,),
        in_specs=[row] * 5 + [_const_spec(c.shape) for c in consts],
        out_specs=row,
        out_shape=jax.ShapeDtypeStruct((m, D_MODEL), F32),
        compiler_params=_params("parallel"),
        name="merge_ffn",
    )(h, oa, ob, ga, gb, *consts)


def _prep_ffn(w_in, w_out):
    return w_in[:, :D_FF].astype(BF16), w_in[:, D_FF:].astype(BF16), w_out.astype(BF16)


def _row(v):
    return v.reshape(1, -1)


def _layer(x, s0, k_past, v_past, wts):
    bsz, t, _ = x.shape
    m = bsz * t
    h = _ffn_ln(x.reshape(m, D_MODEL), *wts["ffn1"])
    gq, gk, gv, gr, la, sq, sk, sv, ga, gb, skb, svb = _proj(h, *wts["proj"])

    def seq(a):
        return a.reshape(bsz, t, a.shape[-1])

    o_a, s_new = _gla(seq(gq), seq(gk), seq(gv), seq(la), seq(gr), wts["gla_norm"], s0)
    if k_past is None:
        o_b = _sb_prompt(seq(sq), seq(skb), seq(svb))
    else:
        o_b = _sb_sample(seq(sq), seq(skb), seq(svb), k_past, v_past)
    y = _merge_ffn(h, o_a.reshape(m, GLA_V), o_b.reshape(m, SB_W), ga, gb, *wts["merge"], *wts["ffn2"])
    return (y.reshape(bsz, t, D_MODEL), s_new,
            sk.reshape(bsz, t, SB_HEADS, SB_DH), sv.reshape(bsz, t, SB_HEADS, SB_DH))


def kernel(x_prompt, x_sample, state_gla, cache_sb_k, cache_sb_v, ffn1_w_in, ffn1_w_out, ln1_g, ln1_b, w_in, w_gla_gate_up, b_gla_gate, g_gla_norm, w_gla_o, w_sb_o, w_out, ln2_g, ln2_b, ffn2_w_in, ffn2_w_out, ln3_g, ln3_b):
    xp, xs = x_prompt, x_sample
    bp = xp.shape[0]
    outs = [[] for _ in range(6)]
    lr_lo = 2 * GLA_QK + 2 * GLA_V
    lr_hi = lr_lo + GLA_GATE_RANK
    for l in range(DEPTH):
        w_lr = jnp.pad(w_in[l][:, lr_lo:lr_hi], ((0, 0), (0, LANES - GLA_GATE_RANK))).astype(BF16)
        w_up = jnp.pad(w_gla_gate_up[l], ((0, LANES - GLA_GATE_RANK), (0, 0))).astype(BF16)
        wts = {
            "ffn1": (*_prep_ffn(ffn1_w_in[l], ffn1_w_out[l]), _row(ln1_g[l]), _row(ln1_b[l])),
            "proj": (w_in[l][:, :lr_lo].astype(BF16), w_in[l][:, lr_hi:].astype(BF16), w_lr, w_up,
                     _row(b_gla_gate[l])),
            "gla_norm": _row(g_gla_norm[l]),
            "merge": (w_gla_o[l].astype(BF16), w_sb_o[l].astype(BF16), w_out[l].astype(BF16),
                      _row(ln2_g[l]), _row(ln2_b[l])),
            "ffn2": (*_prep_ffn(ffn2_w_in[l], ffn2_w_out[l]), _row(ln3_g[l]), _row(ln3_b[l])),
        }
        s0_p = jnp.zeros((bp, GLA_HEADS, GLA_DK, GLA_DV), F32)
        xp, sp, kp, vp = _layer(xp, s0_p, None, None, wts)
        xs, ss, ksn, vsn = _layer(xs, state_gla[l], cache_sb_k[l], cache_sb_v[l], wts)
        for lst, val in zip(outs, (sp, kp, vp, ss, ksn, vsn)):
            lst.append(val)
    return (xp, xs, *(jnp.stack(o) for o in outs))
```

```python
import functools

import jax
import jax.numpy as jnp
from jax import lax
from jax.experimental import pallas as pl
from jax.experimental.pallas import tpu as pltpu

F32 = jnp.float32
BF16 = jnp.bfloat16

D_MODEL = 1024
CHUNK = 64
GLA_HEADS = 4
GLA_DK = 128
GLA_DV = 256
GLA_QK = GLA_HEADS * GLA_DK
GLA_V = GLA_HEADS * GLA_DV
GLA_GATE_RANK = 16
GLA_GATE_TEMP = 16.0
SB_HEADS = 8
SB_DH = 128
SB_W = SB_HEADS * SB_DH
D_FF = 2816
LN_EPS = 1e-5
DEPTH = 1
DN_ALPHA = (2 * DEPTH) ** 0.25

LOG2E = 1.4426950408889634
LANES = 128
GLA_ROWS = 512
GLA_BATCHED_ROWS = 256
ROW_TILE = 512
PROJ_ROW_TILE = 256
CAST_ROWS = 64
TAIL_ROW_TILE = 256
SB_BLOCK = 256
SB_QBLOCK = 256
SB_LOOKBACK = 1
SB_GROUP = 8
SB_DEAD = 110.0
VMEM_LIMIT = 56 * 1024 * 1024


def _layer_norm(r, g, b):
    mu = jnp.mean(r, axis=-1, keepdims=True)
    d = r - mu
    var = jnp.mean(d * d, axis=-1, keepdims=True)
    return d * lax.rsqrt(var + LN_EPS) * g + b


def _softplus_neg_abs(z):
    return jnp.log(1.0 + jnp.exp(-jnp.abs(z)))


def _split3(x):
    x1 = x.astype(BF16)
    r1 = x - x1.astype(F32)
    x2 = r1.astype(BF16)
    r2 = r1 - x2.astype(F32)
    return x1, x2, r2.astype(BF16)


def _dot(a, b):
    return jnp.dot(a, b, preferred_element_type=F32)


def _dot_nt(a, b):
    return lax.dot_general(a, b, (((1,), (1,)), ((), ())), preferred_element_type=F32)


def _dot_tn(a, b):
    return lax.dot_general(a, b, (((0,), (0,)), ((), ())), preferred_element_type=F32)


def _const_spec(shape):
    nd = len(shape)
    return pl.BlockSpec(shape, lambda *_: (0,) * nd, pipeline_mode=pl.Buffered(1))


def _params(*sem):
    return pltpu.CompilerParams(dimension_semantics=sem, vmem_limit_bytes=VMEM_LIMIT)


def _swiglu_ln(x, wg_ref, wu_ref, wo_ref, g_ref, b_ref):
    xb = x.astype(BF16)
    gate = _dot(xb, wg_ref[...])
    up = _dot(xb, wu_ref[...])
    a = (gate * jax.nn.sigmoid(gate) * up).astype(BF16)
    return _layer_norm(DN_ALPHA * x + 0.5 * _dot(a, wo_ref[...]), g_ref[...], b_ref[...])


def _ffn_ln_kernel(x_ref, wg_ref, wu_ref, wo_ref, g_ref, b_ref, o_ref):
    o_ref[...] = _swiglu_ln(x_ref[...], wg_ref, wu_ref, wo_ref, g_ref, b_ref)


def _ffn_ln(x, wg, wu, wo, g, b):
    m = x.shape[0]
    tm = min(ROW_TILE, m)
    row = pl.BlockSpec((tm, D_MODEL), lambda i: (i, 0))
    return pl.pallas_call(
        _ffn_ln_kernel,
        grid=(m // tm,),
        in_specs=[row, _const_spec(wg.shape), _const_spec(wu.shape), _const_spec(wo.shape),
                  _const_spec(g.shape), _const_spec(b.shape)],
        out_specs=row,
        out_shape=jax.ShapeDtypeStruct((m, D_MODEL), F32),
        compiler_params=_params("parallel"),
        name="ffn_ln",
    )(x, wg, wu, wo, g, b)


_PROJ_GROUPS_A = (("gq", GLA_QK), ("gk", GLA_QK), ("gv", GLA_V), ("gr", GLA_V))
_PROJ_GROUPS_B = (("sq", SB_W), ("sk", SB_W), ("sv", SB_W), ("ga", D_MODEL), ("gb", D_MODEL))


def _proj_kernel(h_ref, wa_ref, wb_ref, wlr_ref, wup_ref, bup_ref,
                 gq_ref, gk_ref, gv_ref, gr_ref, la_ref, sq_ref, sk_ref, sv_ref, ga_ref, gb_ref,
                 skb_ref, svb_ref):
    hb = h_ref[...].astype(BF16)
    outs = dict(gq=gq_ref, gk=gk_ref, gv=gv_ref, gr=gr_ref, sq=sq_ref, sk=sk_ref, sv=sv_ref,
                ga=ga_ref, gb=gb_ref)
    for w_ref, groups in ((wa_ref, _PROJ_GROUPS_A), (wb_ref, _PROJ_GROUPS_B)):
        lo = 0
        for name, width in groups:
            y = _dot(hb, w_ref[:, lo:lo + width])
            lo += width
            if name == "gq":
                y = y * (GLA_DK ** -0.5)
            if name == "sq":
                y = y * (SB_DH ** -0.5)
            if name in ("sq", "gv"):
                outs[name][...] = y.astype(BF16)
            else:
                outs[name][...] = y
            if name == "sk":
                skb_ref[...] = y.astype(BF16)
            if name == "sv":
                svb_ref[...] = y.astype(BF16)
    lr = _dot(hb, wlr_ref[...]).astype(BF16)
    pre = _dot(lr, wup_ref[...]) + bup_ref[...]
    la_ref[...] = (jnp.minimum(pre, 0.0) - _softplus_neg_abs(pre)) * (1.0 / GLA_GATE_TEMP)


def _proj(h, w_a, w_b, w_lr, w_up, b_up):
    m = h.shape[0]
    tm = min(PROJ_ROW_TILE, m)

    def row(width):
        return pl.BlockSpec((tm, width), lambda i: (i, 0))

    widths = (GLA_QK, GLA_QK, GLA_V, GLA_V, GLA_QK, SB_W, SB_W, SB_W, D_MODEL, D_MODEL, SB_W, SB_W)
    dtypes = (F32, F32, BF16, F32, F32, BF16, F32, F32, F32, F32, BF16, BF16)
    consts = (w_a, w_b, w_lr, w_up, b_up)
    return pl.pallas_call(
        _proj_kernel,
        grid=(m // tm,),
        in_specs=[row(D_MODEL)] + [_const_spec(c.shape) for c in consts],
        out_specs=[row(w) for w in widths],
        out_shape=[jax.ShapeDtypeStruct((m, w), d) for w, d in zip(widths, dtypes)],
        compiler_params=_params("parallel"),
        name="proj",
    )(h, *consts)


def _gla_cumsum_matrix(tb):
    r = lax.broadcasted_iota(jnp.int32, (2 * tb, tb), 0)
    c = lax.broadcasted_iota(jnp.int32, (2 * tb, tb), 1)
    same_chunk = (r % tb) // CHUNK == c // CHUNK
    return (same_chunk & ((c <= r) | (r >= tb))).astype(BF16)


def _gla_kernel(q_ref, k_ref, v_ref, la_ref, gr_ref, gn_ref, s0_ref, cm_ref, o_ref, sout_ref, s_scr):
    t = pl.program_id(1)

    @pl.when(t == 0)
    def _():
        s_scr[...] = s0_ref[...]

    nb, tb, _ = q_ref.shape
    sb = cm_ref.shape[1]
    cm = cm_ref[...]
    gn = gn_ref[...]
    rows = lax.broadcasted_iota(jnp.int32, (sb, sb), 0)
    cols = lax.broadcasted_iota(jnp.int32, (sb, sb), 1)
    shift = CHUNK.bit_length() - 1
    causal = (cols <= rows) & ((rows >> shift) == (cols >> shift))

    for bi in range(nb):
        for r0 in range(0, tb, sb):
            rb = slice(r0, r0 + sb)
            for h in range(GLA_HEADS):
                ks = slice(h * GLA_DK, (h + 1) * GLA_DK)
                vs = slice(h * GLA_DV, (h + 1) * GLA_DV)
                cs = _dot(cm, jnp.concatenate(_split3(la_ref[bi, rb, ks]), axis=1))
                cs = cs[:, :GLA_DK] + cs[:, GLA_DK:2 * GLA_DK] + cs[:, 2 * GLA_DK:]
                b, b_tot = cs[:sb], cs[sb:]
                q = q_ref[bi, rb, ks]
                k = k_ref[bi, rb, ks]
                v = v_ref[bi, rb, vs]
                qg = (q * jnp.exp(b)).astype(BF16)
                kg = (k * jnp.exp(-b)).astype(BF16)
                kd = (k * jnp.exp(b_tot - b)).astype(BF16)
                att = jnp.where(causal, _dot_nt(qg, kg), 0.0).astype(BF16)
                o_intra = _dot(att, v)
                s = s_scr[bi, h]
                for c in range(sb // CHUNK):
                    rs = slice(c * CHUNK, (c + 1) * CHUNK)
                    ro = slice(r0 + c * CHUNK, r0 + (c + 1) * CHUNK)
                    o = o_intra[rs] + _dot(qg[rs], s.astype(BF16))
                    o = o * lax.rsqrt(jnp.mean(o * o, axis=-1, keepdims=True) + LN_EPS) * gn[:, vs]
                    gate = gr_ref[bi, ro, vs]
                    o_ref[bi, ro, vs] = (o * (gate * jax.nn.sigmoid(gate))).astype(BF16)
                    decay = jnp.exp(jnp.broadcast_to(b_tot[c * CHUNK:c * CHUNK + 1], (GLA_DK, GLA_DK))).T
                    s = jnp.concatenate([decay, decay], axis=1) * s + _dot_tn(kd[rs], v[rs])
                s_scr[bi, h] = s

    @pl.when(t == pl.num_programs(1) - 1)
    def _():
        sout_ref[...] = s_scr[...]


def _gla(q, k, v, la, gr, gn, s0):
    bsz, t, _ = q.shape
    tb = min(GLA_ROWS, t)
    nb = max(1, min(bsz, GLA_ROWS // t))
    assert t % tb == 0 and bsz % nb == 0 and tb % CHUNK == 0
    cm = _gla_cumsum_matrix(min(tb, GLA_BATCHED_ROWS))
    qk_spec = pl.BlockSpec((nb, tb, GLA_QK), lambda b, i: (b, i, 0))
    v_spec = pl.BlockSpec((nb, tb, GLA_V), lambda b, i: (b, i, 0))
    s_spec = pl.BlockSpec((nb, GLA_HEADS, GLA_DK, GLA_DV), lambda b, i: (b, 0, 0, 0))
    return pl.pallas_call(
        _gla_kernel,
        grid=(bsz // nb, t // tb),
        in_specs=[qk_spec, qk_spec, v_spec, qk_spec, v_spec, _const_spec(gn.shape), s_spec,
                  _const_spec(cm.shape)],
        out_specs=[v_spec, s_spec],
        out_shape=[jax.ShapeDtypeStruct((bsz, t, GLA_V), BF16),
                   jax.ShapeDtypeStruct((bsz, GLA_HEADS, GLA_DK, GLA_DV), F32)],
        scratch_shapes=[pltpu.VMEM((nb, GLA_HEADS, GLA_DK, GLA_DV), F32)],
        compiler_params=_params("parallel", "arbitrary"),
        name="gla",
    )(q, k, v, la, gr, gn, s0, cm)


def _suffix_matrix(tk):
    r = lax.broadcasted_iota(jnp.int32, (2 * tk, tk), 0) % tk
    c = lax.broadcasted_iota(jnp.int32, (2 * tk, tk), 1)
    return (r >= c).astype(BF16)


def _sb_update(qs, ks, vs, carry_ref, acc_ref, suffix, visible, first):
    heads = range(len(qs))
    sub = suffix.shape[1]
    n_sub = ks[0].shape[0] // sub
    z = [_dot_nt(qs[hd], ks[hd]) for hd in heads]
    carry = [None if first else carry_ref[hd] for hd in heads]
    ws = [[None] * n_sub for _ in heads]
    for s in reversed(range(n_sub)):
        vis = None if visible is None else visible[s]
        zs = [z[hd][:, s * sub:(s + 1) * sub] for hd in heads]
        if vis is not None:
            zs = [jnp.where(vis, x, -jnp.inf) for x in zs]
        cs = [jnp.maximum(x, 0.0) + jnp.log(1.0 + jnp.exp2(jnp.abs(x) * -LOG2E)) for x in zs]
        his = [c.astype(BF16) for c in cs]
        los = [(c - hi.astype(F32)).astype(BF16) for c, hi in zip(cs, his)]
        sums = [_dot(jnp.concatenate([hi, lo], axis=1), suffix) for hi, lo in zip(his, los)]
        for hd in heads:
            e = zs[hd] - sums[hd]
            if carry[hd] is not None:
                e = e - jnp.concatenate([carry[hd]] * (sub // LANES), axis=1)
            ws[hd][s] = jnp.exp(e).astype(BF16)
            total = jnp.broadcast_to(jnp.sum(cs[hd], axis=-1, keepdims=True), (cs[hd].shape[0], LANES))
            carry[hd] = total if carry[hd] is None else carry[hd] + total
    for hd in heads:
        pv = _dot(ws[hd][0] if n_sub == 1 else jnp.concatenate(ws[hd], axis=1), vs[hd])
        acc_ref[hd] = pv if first else acc_ref[hd] + pv
        carry_ref[hd] = carry[hd]


def _head(hd):
    return slice(hd * SB_DH, (hd + 1) * SB_DH)


def _sb_prompt_kernel(q_ref, k_ref, v_ref, sfx_ref, o_ref, carry_scr, acc_scr):
    i = pl.program_id(2)
    tq = q_ref.shape[1]
    heads = q_ref.shape[2] // SB_DH
    sfx = sfx_ref[...]
    rows = lax.broadcasted_iota(jnp.int32, (tq, tq), 0)
    cols = lax.broadcasted_iota(jnp.int32, (tq, tq), 1)
    strictly_earlier = cols < rows
    qs = [q_ref[0, :, _head(hd)] for hd in range(heads)]

    def keys(ref, start, n):
        return [ref[0, pl.ds(start, n), _head(hd)] for hd in range(heads)]

    for n_prev in range(SB_LOOKBACK + 1):
        @pl.when((i == n_prev) if n_prev < SB_LOOKBACK else (i >= n_prev))
        def _(n_prev=n_prev):
            p0 = pl.multiple_of((i - n_prev) * tq, tq)
            n = (n_prev + 1) * tq
            _sb_update(qs, keys(k_ref, p0, n), keys(v_ref, p0, n), carry_scr, acc_scr, sfx,
                       (None,) * n_prev + (strictly_earlier,), True)

    n_older = jnp.maximum(i - SB_LOOKBACK, 0)

    def cond(state):
        j, cmin = state
        return (j < n_older) & (cmin <= SB_DEAD)

    def body(state):
        j, _ = state
        k0 = pl.multiple_of((i - SB_LOOKBACK - 1 - j) * tq, tq)
        _sb_update(qs, keys(k_ref, k0, tq), keys(v_ref, k0, tq), carry_scr, acc_scr, sfx, None, False)
        return j + 1, jnp.min(carry_scr[...])

    lax.while_loop(cond, body, (jnp.int32(0), jnp.min(carry_scr[...])))
    for hd in range(heads):
        o_ref[0, :, _head(hd)] = acc_scr[hd].astype(o_ref.dtype)


def _sb_prompt(sq, sk, sv):
    bsz, t, _ = sq.shape
    blk = SB_QBLOCK
    assert t % blk == 0
    width = SB_GROUP * SB_DH
    sfx = _suffix_matrix(blk)
    q_spec = pl.BlockSpec((1, blk, width), lambda b, g, i: (b, i, g))
    kv_spec = pl.BlockSpec((1, t, width), lambda b, g, i: (b, 0, g), pipeline_mode=pl.Buffered(1))
    return pl.pallas_call(
        _sb_prompt_kernel,
        grid=(bsz, SB_HEADS // SB_GROUP, t // blk),
        in_specs=[q_spec, kv_spec, kv_spec, _const_spec(sfx.shape)],
        out_specs=q_spec,
        out_shape=jax.ShapeDtypeStruct((bsz, t, SB_W), BF16),
        scratch_shapes=[pltpu.VMEM((SB_GROUP, blk, LANES), F32), pltpu.VMEM((SB_GROUP, blk, SB_DH), F32)],
        compiler_params=_params("parallel", "parallel", "arbitrary"),
        name="sb_prompt",
    )(sq, sk, sv, sfx)


def _sb_sample_kernel(q_ref, kn_ref, vn_ref, kw_ref, vw_ref, k_hbm, v_hbm, sfx_ref, o_ref,
                      carry_scr, acc_scr, kbuf, vbuf, sem, *, n_past):
    b = pl.program_id(0)
    tq = q_ref.shape[1]
    blk = SB_BLOCK
    win = blk - tq
    sfx = sfx_ref[...]
    rows = lax.broadcasted_iota(jnp.int32, (tq, blk), 0)
    cols = lax.broadcasted_iota(jnp.int32, (tq, blk), 1)

    def head_rows(ref, hd, first, n):
        return ref[pl.ds(first * SB_HEADS + hd, n, stride=SB_HEADS), :].astype(BF16)

    visible = (cols < win) | (cols - win < rows)
    qs = [q_ref[0, :, _head(hd)] for hd in range(SB_HEADS)]
    _sb_update(qs,
               [jnp.concatenate([head_rows(kw_ref.at[0], hd, blk - win, win), kn_ref[0, :, _head(hd)]], axis=0)
                for hd in range(SB_HEADS)],
               [jnp.concatenate([head_rows(vw_ref.at[0], hd, blk - win, win), vn_ref[0, :, _head(hd)]], axis=0)
                for hd in range(SB_HEADS)],
               carry_scr, acc_scr, sfx, (visible,), True)

    n_left = n_past - win
    n_older = -(-n_left // blk)

    def cond(state):
        j, cmin = state
        return (j < n_older) & (cmin <= SB_DEAD)

    def body(state):
        j, _ = state
        end = n_left - j * blk
        start = jnp.maximum(end - blk, 0)
        row0 = pl.multiple_of(start * SB_HEADS, SB_HEADS)
        copy_k = pltpu.make_async_copy(k_hbm.at[b, pl.ds(row0, blk * SB_HEADS)], kbuf, sem.at[0])
        copy_v = pltpu.make_async_copy(v_hbm.at[b, pl.ds(row0, blk * SB_HEADS)], vbuf, sem.at[1])
        copy_k.start()
        copy_v.start()
        copy_k.wait()
        copy_v.wait()
        unread = cols < end - start
        _sb_update(qs, [head_rows(kbuf, hd, 0, blk) for hd in range(SB_HEADS)],
                   [head_rows(vbuf, hd, 0, blk) for hd in range(SB_HEADS)],
                   carry_scr, acc_scr, sfx, (unread,), False)
        return j + 1, jnp.min(carry_scr[...])

    lax.while_loop(cond, body, (jnp.int32(0), jnp.min(carry_scr[...])))
    for hd in range(SB_HEADS):
        o_ref[0, :, _head(hd)] = acc_scr[hd].astype(o_ref.dtype)


def _sb_sample(sq, sk, sv, k_past, v_past):
    bsz, t, _ = sq.shape
    p = k_past.shape[1]
    blk = SB_BLOCK
    assert p % blk == 0 and p >= blk and t < blk and t % 16 == 0
    k_past = k_past.reshape(bsz, p * SB_HEADS, SB_DH)
    v_past = v_past.reshape(bsz, p * SB_HEADS, SB_DH)
    sfx = _suffix_matrix(blk)
    new_spec = pl.BlockSpec((1, t, SB_W), lambda b: (b, 0, 0))
    window_spec = pl.BlockSpec((1, blk * SB_HEADS, SB_DH), lambda b: (b, p // blk - 1, 0))
    hbm_spec = pl.BlockSpec(memory_space=pl.ANY)
    return pl.pallas_call(
        functools.partial(_sb_sample_kernel, n_past=p),
        grid=(bsz,),
        in_specs=[new_spec, new_spec, new_spec, window_spec, window_spec, hbm_spec, hbm_spec,
                  _const_spec(sfx.shape)],
        out_specs=new_spec,
        out_shape=jax.ShapeDtypeStruct((bsz, t, SB_W), BF16),
        scratch_shapes=[pltpu.VMEM((SB_HEADS, t, LANES), F32), pltpu.VMEM((SB_HEADS, t, SB_DH), F32),
                        pltpu.VMEM((blk * SB_HEADS, SB_DH), F32), pltpu.VMEM((blk * SB_HEADS, SB_DH), F32),
                        pltpu.SemaphoreType.DMA((2,))],
        compiler_params=_params("arbitrary"),
        name="sb_sample",
    )(sq, sk, sv, k_past, v_past, k_past, v_past, sfx)


def _merge_ffn_kernel(h_ref, oa_ref, ob_ref, ga_ref, gb_ref, wa_ref, wb_ref, wm_ref, g2_ref, b2_ref,
                      wg_ref, wu_ref, wo_ref, g3_ref, b3_ref, o_ref):
    branch_a = _dot(oa_ref[...], wa_ref[...])
    branch_b = _dot(ob_ref[...], wb_ref[...])
    merged = jax.nn.sigmoid(ga_ref[...]) * branch_a + jax.nn.sigmoid(gb_ref[...]) * branch_b
    mix = _dot(merged.astype(BF16), wm_ref[...])
    h2 = _layer_norm(DN_ALPHA * h_ref[...] + mix, g2_ref[...], b2_ref[...])
    o_ref[...] = _swiglu_ln(h2, wg_ref, wu_ref, wo_ref, g3_ref, b3_ref)


def _merge_ffn(h, oa, ob, ga, gb, *consts):
    m = h.shape[0]
    tm = min(TAIL_ROW_TILE, m)
    row = pl.BlockSpec((tm, D_MODEL), lambda i: (i, 0))
    return pl.pallas_call(
        _merge_ffn_kernel,
        grid=(m // tm,),
        in_specs=[row] * 5 + [_const_spec(c.shape) for c in consts],
        out_specs=row,
        out_shape=jax.ShapeDtypeStruct((m, D_MODEL), F32),
        compiler_params=_params("parallel"),
        name="merge_ffn",
    )(h, oa, ob, ga, gb, *consts)


def _cast_kernel(*refs, plans):
    n_in = len(plans)
    outs = iter(refs[n_in:])
    for ref, plan in zip(refs[:n_in], plans):
        x = ref[...]
        for lo, hi, keep in plan:
            y = x[:, lo:hi]
            if keep < hi - lo:
                y = jnp.where(lax.broadcasted_iota(jnp.int32, y.shape, 1) < keep, y, 0.0)
            next(outs)[...] = y.astype(BF16)


def _cast_weights(arrays, plans, rows_per_step):
    rows = arrays[0].shape[0]
    assert all(a.shape[0] == rows for a in arrays) and rows % rows_per_step == 0
    out_shape = [jax.ShapeDtypeStruct((rows, hi - lo), BF16) for plan in plans for lo, hi, _ in plan]
    return pl.pallas_call(
        functools.partial(_cast_kernel, plans=plans),
        grid=(rows // rows_per_step,),
        in_specs=[pl.BlockSpec((rows_per_step, a.shape[1]), lambda i: (i, 0)) for a in arrays],
        out_specs=[pl.BlockSpec((rows_per_step, o.shape[1]), lambda i: (i, 0)) for o in out_shape],
        out_shape=out_shape,
        compiler_params=_params("parallel"),
        name="cast_weights",
    )(*arrays)


def _row(v):
    return v.reshape(1, -1)


def _layer(x, s0, k_past, v_past, wts):
    bsz, t, _ = x.shape
    m = bsz * t
    h = _ffn_ln(x.reshape(m, D_MODEL), *wts["ffn1"])
    gq, gk, gv, gr, la, sq, sk, sv, ga, gb, skb, svb = _proj(h, *wts["proj"])

    def seq(a):
        return a.reshape(bsz, t, a.shape[-1])

    o_a, s_new = _gla(seq(gq), seq(gk), seq(gv), seq(la), seq(gr), wts["gla_norm"], s0)
    if k_past is None:
        o_b = _sb_prompt(seq(sq), seq(skb), seq(svb))
    else:
        o_b = _sb_sample(seq(sq), seq(skb), seq(svb), k_past, v_past)
    y = _merge_ffn(h, o_a.reshape(m, GLA_V), o_b.reshape(m, SB_W), ga, gb, *wts["merge"], *wts["ffn2"])
    return (y.reshape(bsz, t, D_MODEL), s_new,
            sk.reshape(bsz, t, SB_HEADS, SB_DH), sv.reshape(bsz, t, SB_HEADS, SB_DH))


def kernel(x_prompt, x_sample, state_gla, cache_sb_k, cache_sb_v, ffn1_w_in, ffn1_w_out, ln1_g, ln1_b, w_in, w_gla_gate_up, b_gla_gate, g_gla_norm, w_gla_o, w_sb_o, w_out, ln2_g, ln2_b, ffn2_w_in, ffn2_w_out, ln3_g, ln3_b):
    xp, xs = x_prompt, x_sample
    bp = xp.shape[0]
    outs = [[] for _ in range(6)]
    lr_lo = 2 * GLA_QK + 2 * GLA_V
    lr_hi = lr_lo + GLA_GATE_RANK
    whole = ((0, D_MODEL, D_MODEL),)
    halves = ((0, D_FF, D_FF), (D_FF, 2 * D_FF, D_FF))
    for l in range(DEPTH):
        (wg1, wu1, wg2, wu2, w_a, w_lr, w_b, w_ga, w_sb, w_o) = _cast_weights(
            [ffn1_w_in[l], ffn2_w_in[l], w_in[l], w_gla_o[l], w_sb_o[l], w_out[l]],
            (halves, halves,
             ((0, lr_lo, lr_lo), (lr_lo, lr_lo + LANES, GLA_GATE_RANK), (lr_hi, w_in.shape[2], w_in.shape[2] - lr_hi)),
             whole, whole, whole),
            CAST_ROWS)
        wo1, wo2 = _cast_weights([ffn1_w_out[l], ffn2_w_out[l]], (whole, whole), CAST_ROWS)
        w_up = jnp.pad(w_gla_gate_up[l], ((0, LANES - GLA_GATE_RANK), (0, 0))).astype(BF16)
        wts = {
            "ffn1": (wg1, wu1, wo1, _row(ln1_g[l]), _row(ln1_b[l])),
            "proj": (w_a, w_b, w_lr, w_up, _row(b_gla_gate[l])),
            "gla_norm": _row(g_gla_norm[l]),
            "merge": (w_ga, w_sb, w_o, _row(ln2_g[l]), _row(ln2_b[l])),
            "ffn2": (wg2, wu2, wo2, _row(ln3_g[l]), _row(ln3_b[l])),
        }
        s0_p = jnp.zeros((bp, GLA_HEADS, GLA_DK, GLA_DV), F32)
        xp, sp, kp, vp = _layer(xp, s0_p, None, None, wts)
        xs, ss, ksn, vsn = _layer(xs, state_gla[l], cache_sb_k[l], cache_sb_v[l], wts)
        for lst, val in zip(outs, (sp, kp, vp, ss, ksn, vsn)):
            lst.append(val)
    return (xp, xs, *(jnp.stack(o) for o in outs))
```

```python
import functools

import jax
import jax.numpy as jnp
from jax import lax
from jax.experimental import pallas as pl
from jax.experimental.pallas import tpu as pltpu

F32 = jnp.float32
BF16 = jnp.bfloat16

D_MODEL = 1024
CHUNK = 64
GLA_HEADS = 4
GLA_DK = 128
GLA_DV = 256
GLA_QK = GLA_HEADS * GLA_DK
GLA_V = GLA_HEADS * GLA_DV
GLA_GATE_RANK = 16
GLA_GATE_TEMP = 16.0
SB_HEADS = 8
SB_DH = 128
SB_W = SB_HEADS * SB_DH
D_FF = 2816
LN_EPS = 1e-5
DEPTH = 1
DN_ALPHA = (2 * DEPTH) ** 0.25

LOG2E = 1.4426950408889634
LANES = 128
GLA_ROWS = 512
GLA_BATCHED_ROWS = 256
ROW_TILE = 512
PROJ_ROW_TILE = 256
CAST_ROWS = 64
TAIL_ROW_TILE = 256
SB_BLOCK = 256
SB_QBLOCK = 256
SB_LOOKBACK = 1
SB_GROUP = 8
SB_DEAD = 110.0
VMEM_LIMIT = 56 * 1024 * 1024


def _layer_norm(r, g, b):
    mu = jnp.mean(r, axis=-1, keepdims=True)
    d = r - mu
    var = jnp.mean(d * d, axis=-1, keepdims=True)
    return d * lax.rsqrt(var + LN_EPS) * g + b


def _softplus_neg_abs(z):
    return jnp.log(1.0 + jnp.exp(-jnp.abs(z)))


def _split3(x):
    x1 = x.astype(BF16)
    r1 = x - x1.astype(F32)
    x2 = r1.astype(BF16)
    r2 = r1 - x2.astype(F32)
    return x1, x2, r2.astype(BF16)


def _dot(a, b):
    return jnp.dot(a, b, preferred_element_type=F32)


def _dot_nt(a, b):
    return lax.dot_general(a, b, (((1,), (1,)), ((), ())), preferred_element_type=F32)


def _dot_tn(a, b):
    return lax.dot_general(a, b, (((0,), (0,)), ((), ())), preferred_element_type=F32)


def _const_spec(shape):
    nd = len(shape)
    return pl.BlockSpec(shape, lambda *_: (0,) * nd, pipeline_mode=pl.Buffered(1))


def _params(*sem):
    return pltpu.CompilerParams(dimension_semantics=sem, vmem_limit_bytes=VMEM_LIMIT)


def _swiglu_ln(x, wg_ref, wu_ref, wo_ref, g_ref, b_ref):
    xb = x.astype(BF16)
    gate = _dot(xb, wg_ref[...])
    up = _dot(xb, wu_ref[...])
    a = (gate * jax.nn.sigmoid(gate) * up).astype(BF16)
    return _layer_norm(DN_ALPHA * x + 0.5 * _dot(a, wo_ref[...]), g_ref[...], b_ref[...])


def _ffn_ln_kernel(x_ref, wg_ref, wu_ref, wo_ref, g_ref, b_ref, o_ref):
    o_ref[...] = _swiglu_ln(x_ref[...], wg_ref, wu_ref, wo_ref, g_ref, b_ref)


def _ffn_ln(x, wg, wu, wo, g, b):
    m = x.shape[0]
    tm = min(ROW_TILE, m)
    row = pl.BlockSpec((tm, D_MODEL), lambda i: (i, 0))
    return pl.pallas_call(
        _ffn_ln_kernel,
        grid=(m // tm,),
        in_specs=[row, _const_spec(wg.shape), _const_spec(wu.shape), _const_spec(wo.shape),
                  _const_spec(g.shape), _const_spec(b.shape)],
        out_specs=row,
        out_shape=jax.ShapeDtypeStruct((m, D_MODEL), F32),
        compiler_params=_params("parallel"),
        name="ffn_ln",
    )(x, wg, wu, wo, g, b)


_PROJ_GROUPS_A = (("gq", GLA_QK), ("gk", GLA_QK), ("gv", GLA_V), ("gr", GLA_V))
_PROJ_GROUPS_B = (("sq", SB_W), ("sk", SB_W), ("sv", SB_W), ("ga", D_MODEL), ("gb", D_MODEL))


def _proj_kernel(h_ref, wa_ref, wb_ref, wlr_ref, wup_ref, bup_ref,
                 gq_ref, gk_ref, gv_ref, gr_ref, la_ref, sq_ref, sk_ref, sv_ref, ga_ref, gb_ref,
                 skb_ref, svb_ref):
    hb = h_ref[...].astype(BF16)
    outs = dict(gq=gq_ref, gk=gk_ref, gv=gv_ref, gr=gr_ref, sq=sq_ref, sk=sk_ref, sv=sv_ref,
                ga=ga_ref, gb=gb_ref)
    for w_ref, groups in ((wa_ref, _PROJ_GROUPS_A), (wb_ref, _PROJ_GROUPS_B)):
        lo = 0
        for name, width in groups:
            y = _dot_nt(hb, w_ref[lo:lo + width, :])
            lo += width
            if name == "gq":
                y = y * (GLA_DK ** -0.5)
            if name == "sq":
                y = y * (SB_DH ** -0.5)
            if name in ("sq", "gv"):
                outs[name][...] = y.astype(BF16)
            else:
                outs[name][...] = y
            if name == "sk":
                skb_ref[...] = y.astype(BF16)
            if name == "sv":
                svb_ref[...] = y.astype(BF16)
    lr = _dot_nt(hb, wlr_ref[...]).astype(BF16)
    pre = _dot(lr, wup_ref[...]) + bup_ref[...]
    la_ref[...] = (jnp.minimum(pre, 0.0) - _softplus_neg_abs(pre)) * (1.0 / GLA_GATE_TEMP)


def _proj(h, w_a, w_b, w_lr, w_up, b_up):
    m = h.shape[0]
    tm = min(PROJ_ROW_TILE, m)

    def row(width):
        return pl.BlockSpec((tm, width), lambda i: (i, 0))

    widths = (GLA_QK, GLA_QK, GLA_V, GLA_V, GLA_QK, SB_W, SB_W, SB_W, D_MODEL, D_MODEL, SB_W, SB_W)
    dtypes = (F32, F32, BF16, F32, F32, BF16, F32, F32, F32, F32, BF16, BF16)
    consts = (w_a, w_b, w_lr, w_up, b_up)
    return pl.pallas_call(
        _proj_kernel,
        grid=(m // tm,),
        in_specs=[row(D_MODEL)] + [_const_spec(c.shape) for c in consts],
        out_specs=[row(w) for w in widths],
        out_shape=[jax.ShapeDtypeStruct((m, w), d) for w, d in zip(widths, dtypes)],
        compiler_params=_params("parallel"),
        name="proj",
    )(h, *consts)


def _gla_cumsum_matrix(tb):
    r = lax.broadcasted_iota(jnp.int32, (2 * tb, tb), 0)
    c = lax.broadcasted_iota(jnp.int32, (2 * tb, tb), 1)
    same_chunk = (r % tb) // CHUNK == c // CHUNK
    return (same_chunk & ((c <= r) | (r >= tb))).astype(BF16)


def _gla_kernel(q_ref, k_ref, v_ref, la_ref, gr_ref, gn_ref, s0_ref, cm_ref, o_ref, sout_ref, s_scr):
    t = pl.program_id(1)

    @pl.when(t == 0)
    def _():
        s_scr[...] = s0_ref[...]

    nb, tb, _ = q_ref.shape
    sb = cm_ref.shape[1]
    cm = cm_ref[...]
    gn = gn_ref[...]
    rows = lax.broadcasted_iota(jnp.int32, (sb, sb), 0)
    cols = lax.broadcasted_iota(jnp.int32, (sb, sb), 1)
    shift = CHUNK.bit_length() - 1
    causal = (cols <= rows) & ((rows >> shift) == (cols >> shift))

    for bi in range(nb):
        for r0 in range(0, tb, sb):
            rb = slice(r0, r0 + sb)
            for h in range(GLA_HEADS):
                ks = slice(h * GLA_DK, (h + 1) * GLA_DK)
                vs = slice(h * GLA_DV, (h + 1) * GLA_DV)
                cs = _dot(cm, jnp.concatenate(_split3(la_ref[bi, rb, ks]), axis=1))
                cs = cs[:, :GLA_DK] + cs[:, GLA_DK:2 * GLA_DK] + cs[:, 2 * GLA_DK:]
                b, b_tot = cs[:sb], cs[sb:]
                q = q_ref[bi, rb, ks]
                k = k_ref[bi, rb, ks]
                v = v_ref[bi, rb, vs]
                qg = (q * jnp.exp(b)).astype(BF16)
                kg = (k * jnp.exp(-b)).astype(BF16)
                kd = (k * jnp.exp(b_tot - b)).astype(BF16)
                att = jnp.where(causal, _dot_nt(qg, kg), 0.0).astype(BF16)
                o_intra = _dot(att, v)
                s = s_scr[bi, h]
                for c in range(sb // CHUNK):
                    rs = slice(c * CHUNK, (c + 1) * CHUNK)
                    ro = slice(r0 + c * CHUNK, r0 + (c + 1) * CHUNK)
                    o = o_intra[rs] + _dot(qg[rs], s.astype(BF16))
                    o = o * lax.rsqrt(jnp.mean(o * o, axis=-1, keepdims=True) + LN_EPS) * gn[:, vs]
                    gate = gr_ref[bi, ro, vs]
                    o_ref[bi, ro, vs] = (o * (gate * jax.nn.sigmoid(gate))).astype(BF16)
                    decay = jnp.exp(jnp.broadcast_to(b_tot[c * CHUNK:c * CHUNK + 1], (GLA_DK, GLA_DK))).T
                    s = jnp.concatenate([decay, decay], axis=1) * s + _dot_tn(kd[rs], v[rs])
                s_scr[bi, h] = s

    @pl.when(t == pl.num_programs(1) - 1)
    def _():
        sout_ref[...] = s_scr[...]


def _gla(q, k, v, la, gr, gn, s0):
    bsz, t, _ = q.shape
    tb = min(GLA_ROWS, t)
    nb = max(1, min(bsz, GLA_ROWS // t))
    assert t % tb == 0 and bsz % nb == 0 and tb % CHUNK == 0
    cm = _gla_cumsum_matrix(min(tb, GLA_BATCHED_ROWS))
    qk_spec = pl.BlockSpec((nb, tb, GLA_QK), lambda b, i: (b, i, 0))
    v_spec = pl.BlockSpec((nb, tb, GLA_V), lambda b, i: (b, i, 0))
    s_spec = pl.BlockSpec((nb, GLA_HEADS, GLA_DK, GLA_DV), lambda b, i: (b, 0, 0, 0))
    return pl.pallas_call(
        _gla_kernel,
        grid=(bsz // nb, t // tb),
        in_specs=[qk_spec, qk_spec, v_spec, qk_spec, v_spec, _const_spec(gn.shape), s_spec,
                  _const_spec(cm.shape)],
        out_specs=[v_spec, s_spec],
        out_shape=[jax.ShapeDtypeStruct((bsz, t, GLA_V), BF16),
                   jax.ShapeDtypeStruct((bsz, GLA_HEADS, GLA_DK, GLA_DV), F32)],
        scratch_shapes=[pltpu.VMEM((nb, GLA_HEADS, GLA_DK, GLA_DV), F32)],
        compiler_params=_params("parallel", "arbitrary"),
        name="gla",
    )(q, k, v, la, gr, gn, s0, cm)


def _suffix_matrix(tk):
    r = lax.broadcasted_iota(jnp.int32, (2 * tk, tk), 0) % tk
    c = lax.broadcasted_iota(jnp.int32, (2 * tk, tk), 1)
    return (r >= c).astype(BF16)


def _sb_update(qs, ks, vs, carry_ref, acc_ref, suffix, visible, first):
    heads = range(len(qs))
    sub = suffix.shape[1]
    n_sub = ks[0].shape[0] // sub
    z = [_dot_nt(qs[hd], ks[hd]) for hd in heads]
    carry = [None if first else carry_ref[hd] for hd in heads]
    ws = [[None] * n_sub for _ in heads]
    for s in reversed(range(n_sub)):
        vis = None if visible is None else visible[s]
        zs = [z[hd][:, s * sub:(s + 1) * sub] for hd in heads]
        if vis is not None:
            zs = [jnp.where(vis, x, -jnp.inf) for x in zs]
        cs = [jnp.maximum(x, 0.0) + jnp.log(1.0 + jnp.exp2(jnp.abs(x) * -LOG2E)) for x in zs]
        his = [c.astype(BF16) for c in cs]
        los = [(c - hi.astype(F32)).astype(BF16) for c, hi in zip(cs, his)]
        sums = [_dot(jnp.concatenate([hi, lo], axis=1), suffix) for hi, lo in zip(his, los)]
        for hd in heads:
            e = zs[hd] - sums[hd]
            if carry[hd] is not None:
                e = e - jnp.concatenate([carry[hd]] * (sub // LANES), axis=1)
            ws[hd][s] = jnp.exp(e).astype(BF16)
            total = jnp.broadcast_to(jnp.sum(cs[hd], axis=-1, keepdims=True), (cs[hd].shape[0], LANES))
            carry[hd] = total if carry[hd] is None else carry[hd] + total
    for hd in heads:
        pv = _dot(ws[hd][0] if n_sub == 1 else jnp.concatenate(ws[hd], axis=1), vs[hd])
        acc_ref[hd] = pv if first else acc_ref[hd] + pv
        carry_ref[hd] = carry[hd]


def _head(hd):
    return slice(hd * SB_DH, (hd + 1) * SB_DH)


def _sb_prompt_kernel(q_ref, k_ref, v_ref, sfx_ref, o_ref, carry_scr, acc_scr):
    i = pl.program_id(2)
    tq = q_ref.shape[1]
    heads = q_ref.shape[2] // SB_DH
    sfx = sfx_ref[...]
    rows = lax.broadcasted_iota(jnp.int32, (tq, tq), 0)
    cols = lax.broadcasted_iota(jnp.int32, (tq, tq), 1)
    strictly_earlier = cols < rows
    qs = [q_ref[0, :, _head(hd)] for hd in range(heads)]

    def keys(ref, start, n):
        return [ref[0, pl.ds(start, n), _head(hd)] for hd in range(heads)]

    for n_prev in range(SB_LOOKBACK + 1):
        @pl.when((i == n_prev) if n_prev < SB_LOOKBACK else (i >= n_prev))
        def _(n_prev=n_prev):
            p0 = pl.multiple_of((i - n_prev) * tq, tq)
            n = (n_prev + 1) * tq
            _sb_update(qs, keys(k_ref, p0, n), keys(v_ref, p0, n), carry_scr, acc_scr, sfx,
                       (None,) * n_prev + (strictly_earlier,), True)

    n_older = jnp.maximum(i - SB_LOOKBACK, 0)

    def cond(state):
        j, cmin = state
        return (j < n_older) & (cmin <= SB_DEAD)

    def body(state):
        j, _ = state
        k0 = pl.multiple_of((i - SB_LOOKBACK - 1 - j) * tq, tq)
        _sb_update(qs, keys(k_ref, k0, tq), keys(v_ref, k0, tq), carry_scr, acc_scr, sfx, None, False)
        return j + 1, jnp.min(carry_scr[...])

    lax.while_loop(cond, body, (jnp.int32(0), jnp.min(carry_scr[...])))
    for hd in range(heads):
        o_ref[0, :, _head(hd)] = acc_scr[hd].astype(o_ref.dtype)


def _sb_prompt(sq, sk, sv):
    bsz, t, _ = sq.shape
    blk = SB_QBLOCK
    assert t % blk == 0
    width = SB_GROUP * SB_DH
    sfx = _suffix_matrix(blk)
    q_spec = pl.BlockSpec((1, blk, width), lambda b, g, i: (b, i, g))
    kv_spec = pl.BlockSpec((1, t, width), lambda b, g, i: (b, 0, g), pipeline_mode=pl.Buffered(1))
    return pl.pallas_call(
        _sb_prompt_kernel,
        grid=(bsz, SB_HEADS // SB_GROUP, t // blk),
        in_specs=[q_spec, kv_spec, kv_spec, _const_spec(sfx.shape)],
        out_specs=q_spec,
        out_shape=jax.ShapeDtypeStruct((bsz, t, SB_W), BF16),
        scratch_shapes=[pltpu.VMEM((SB_GROUP, blk, LANES), F32), pltpu.VMEM((SB_GROUP, blk, SB_DH), F32)],
        compiler_params=_params("parallel", "parallel", "arbitrary"),
        name="sb_prompt",
    )(sq, sk, sv, sfx)


def _sb_sample_kernel(q_ref, kn_ref, vn_ref, kw_ref, vw_ref, k_hbm, v_hbm, sfx_ref, o_ref,
                      carry_scr, acc_scr, kbuf, vbuf, sem, *, n_past):
    b = pl.program_id(0)
    tq = q_ref.shape[1]
    blk = SB_BLOCK
    win = blk - tq
    sfx = sfx_ref[...]
    rows = lax.broadcasted_iota(jnp.int32, (tq, blk), 0)
    cols = lax.broadcasted_iota(jnp.int32, (tq, blk), 1)

    def head_rows(ref, hd, first, n):
        return ref[pl.ds(first * SB_HEADS + hd, n, stride=SB_HEADS), :].astype(BF16)

    visible = (cols < win) | (cols - win < rows)
    qs = [q_ref[0, :, _head(hd)] for hd in range(SB_HEADS)]
    _sb_update(qs,
               [jnp.concatenate([head_rows(kw_ref.at[0], hd, blk - win, win), kn_ref[0, :, _head(hd)]], axis=0)
                for hd in range(SB_HEADS)],
               [jnp.concatenate([head_rows(vw_ref.at[0], hd, blk - win, win), vn_ref[0, :, _head(hd)]], axis=0)
                for hd in range(SB_HEADS)],
               carry_scr, acc_scr, sfx, (visible,), True)

    n_left = n_past - win
    n_older = -(-n_left // blk)

    def cond(state):
        j, cmin = state
        return (j < n_older) & (cmin <= SB_DEAD)

    def body(state):
        j, _ = state
        end = n_left - j * blk
        start = jnp.maximum(end - blk, 0)
        row0 = pl.multiple_of(start * SB_HEADS, SB_HEADS)
        copy_k = pltpu.make_async_copy(k_hbm.at[b, pl.ds(row0, blk * SB_HEADS)], kbuf, sem.at[0])
        copy_v = pltpu.make_async_copy(v_hbm.at[b, pl.ds(row0, blk * SB_HEADS)], vbuf, sem.at[1])
        copy_k.start()
        copy_v.start()
        copy_k.wait()
        copy_v.wait()
        unread = cols < end - start
        _sb_update(qs, [head_rows(kbuf, hd, 0, blk) for hd in range(SB_HEADS)],
                   [head_rows(vbuf, hd, 0, blk) for hd in range(SB_HEADS)],
                   carry_scr, acc_scr, sfx, (unread,), False)
        return j + 1, jnp.min(carry_scr[...])

    lax.while_loop(cond, body, (jnp.int32(0), jnp.min(carry_scr[...])))
    for hd in range(SB_HEADS):
        o_ref[0, :, _head(hd)] = acc_scr[hd].astype(o_ref.dtype)


def _sb_sample(sq, sk, sv, k_past, v_past):
    bsz, t, _ = sq.shape
    p = k_past.shape[1]
    blk = SB_BLOCK
    assert p % blk == 0 and p >= blk and t < blk and t % 16 == 0
    k_past = k_past.reshape(bsz, p * SB_HEADS, SB_DH)
    v_past = v_past.reshape(bsz, p * SB_HEADS, SB_DH)
    sfx = _suffix_matrix(blk)
    new_spec = pl.BlockSpec((1, t, SB_W), lambda b: (b, 0, 0))
    window_spec = pl.BlockSpec((1, blk * SB_HEADS, SB_DH), lambda b: (b, p // blk - 1, 0))
    hbm_spec = pl.BlockSpec(memory_space=pl.ANY)
    return pl.pallas_call(
        functools.partial(_sb_sample_kernel, n_past=p),
        grid=(bsz,),
        in_specs=[new_spec, new_spec, new_spec, window_spec, window_spec, hbm_spec, hbm_spec,
                  _const_spec(sfx.shape)],
        out_specs=new_spec,
        out_shape=jax.ShapeDtypeStruct((bsz, t, SB_W), BF16),
        scratch_shapes=[pltpu.VMEM((SB_HEADS, t, LANES), F32), pltpu.VMEM((SB_HEADS, t, SB_DH), F32),
                        pltpu.VMEM((blk * SB_HEADS, SB_DH), F32), pltpu.VMEM((blk * SB_HEADS, SB_DH), F32),
                        pltpu.SemaphoreType.DMA((2,))],
        compiler_params=_params("arbitrary"),
        name="sb_sample",
    )(sq, sk, sv, k_past, v_past, k_past, v_past, sfx)


def _merge_ffn_kernel(h_ref, oa_ref, ob_ref, ga_ref, gb_ref, wa_ref, wb_ref, wm_ref, g2_ref, b2_ref,
                      wg_ref, wu_ref, wo_ref, g3_ref, b3_ref, o_ref):
    branch_a = _dot(oa_ref[...], wa_ref[...])
    branch_b = _dot(ob_ref[...], wb_ref[...])
    merged = jax.nn.sigmoid(ga_ref[...]) * branch_a + jax.nn.sigmoid(gb_ref[...]) * branch_b
    mix = _dot(merged.astype(BF16), wm_ref[...])
    h2 = _layer_norm(DN_ALPHA * h_ref[...] + mix, g2_ref[...], b2_ref[...])
    o_ref[...] = _swiglu_ln(h2, wg_ref, wu_ref, wo_ref, g3_ref, b3_ref)


def _merge_ffn(h, oa, ob, ga, gb, *consts):
    m = h.shape[0]
    tm = min(TAIL_ROW_TILE, m)
    row = pl.BlockSpec((tm, D_MODEL), lambda i: (i, 0))
    return pl.pallas_call(
        _merge_ffn_kernel,
        grid=(m // tm,),
        in_specs=[row] * 5 + [_const_spec(c.shape) for c in consts],
        out_specs=row,
        out_shape=jax.ShapeDtypeStruct((m, D_MODEL), F32),
        compiler_params=_params("parallel"),
        name="merge_ffn",
    )(h, oa, ob, ga, gb, *consts)


def _cast_kernel(*refs, plans):
    n_in = len(plans)
    outs = iter(refs[n_in:])
    for ref, plan in zip(refs[:n_in], plans):
        x = ref[...]
        for lo, hi, keep in plan:
            y = x[:, lo:hi]
            if keep < hi - lo:
                y = jnp.where(lax.broadcasted_iota(jnp.int32, y.shape, 1) < keep, y, 0.0)
            next(outs)[...] = y.astype(BF16)


def _cast_weights(arrays, plans, rows_per_step):
    rows = arrays[0].shape[0]
    assert all(a.shape[0] == rows for a in arrays) and rows % rows_per_step == 0
    out_shape = [jax.ShapeDtypeStruct((rows, hi - lo), BF16) for plan in plans for lo, hi, _ in plan]
    return pl.pallas_call(
        functools.partial(_cast_kernel, plans=plans),
        grid=(rows // rows_per_step,),
        in_specs=[pl.BlockSpec((rows_per_step, a.shape[1]), lambda i: (i, 0)) for a in arrays],
        out_specs=[pl.BlockSpec((rows_per_step, o.shape[1]), lambda i: (i, 0)) for o in out_shape],
        out_shape=out_shape,
        compiler_params=_params("parallel"),
        name="cast_weights",
    )(*arrays)


def _row(v):
    return v.reshape(1, -1)


def _layer(x, s0, k_past, v_past, wts):
    bsz, t, _ = x.shape
    m = bsz * t
    h = _ffn_ln(x.reshape(m, D_MODEL), *wts["ffn1"])
    gq, gk, gv, gr, la, sq, sk, sv, ga, gb, skb, svb = _proj(h, *wts["proj"])

    def seq(a):
        return a.reshape(bsz, t, a.shape[-1])

    o_a, s_new = _gla(seq(gq), seq(gk), seq(gv), seq(la), seq(gr), wts["gla_norm"], s0)
    if k_past is None:
        o_b = _sb_prompt(seq(sq), seq(skb), seq(svb))
    else:
        o_b = _sb_sample(seq(sq), seq(skb), seq(svb), k_past, v_past)
    y = _merge_ffn(h, o_a.reshape(m, GLA_V), o_b.reshape(m, SB_W), ga, gb, *wts["merge"], *wts["ffn2"])
    return (y.reshape(bsz, t, D_MODEL), s_new,
            sk.reshape(bsz, t, SB_HEADS, SB_DH), sv.reshape(bsz, t, SB_HEADS, SB_DH))


def kernel(x_prompt, x_sample, state_gla, cache_sb_k, cache_sb_v, ffn1_w_in, ffn1_w_out, ln1_g, ln1_b, w_in, w_gla_gate_up, b_gla_gate, g_gla_norm, w_gla_o, w_sb_o, w_out, ln2_g, ln2_b, ffn2_w_in, ffn2_w_out, ln3_g, ln3_b):
    xp, xs = x_prompt, x_sample
    bp = xp.shape[0]
    outs = [[] for _ in range(6)]
    lr_lo = 2 * GLA_QK + 2 * GLA_V
    lr_hi = lr_lo + GLA_GATE_RANK
    whole = ((0, D_MODEL, D_MODEL),)
    halves = ((0, D_FF, D_FF), (D_FF, 2 * D_FF, D_FF))
    for l in range(DEPTH):
        wg1, wu1, wg2, wu2, w_ga, w_sb, w_o = _cast_weights(
            [ffn1_w_in[l], ffn2_w_in[l], w_gla_o[l], w_sb_o[l], w_out[l]],
            (halves, halves, whole, whole, whole), CAST_ROWS)
        w_in_t = jnp.swapaxes(w_in[l], 0, 1)
        w_a = w_in_t[:lr_lo].astype(BF16)
        w_lr = jnp.pad(w_in_t[lr_lo:lr_hi], ((0, LANES - GLA_GATE_RANK), (0, 0))).astype(BF16)
        w_b = w_in_t[lr_hi:].astype(BF16)
        wo1, wo2 = _cast_weights([ffn1_w_out[l], ffn2_w_out[l]], (whole, whole), D_FF // 4)
        w_up = jnp.pad(w_gla_gate_up[l], ((0, LANES - GLA_GATE_RANK), (0, 0))).astype(BF16)
        wts = {
            "ffn1": (wg1, wu1, wo1, _row(ln1_g[l]), _row(ln1_b[l])),
            "proj": (w_a, w_b, w_lr, w_up, _row(b_gla_gate[l])),
            "gla_norm": _row(g_gla_norm[l]),
            "merge": (w_ga, w_sb, w_o, _row(ln2_g[l]), _row(ln2_b[l])),
            "ffn2": (wg2, wu2, wo2, _row(ln3_g[l]), _row(ln3_b[l])),
        }
        s0_p = jnp.zeros((bp, GLA_HEADS, GLA_DK, GLA_DV), F32)
        xp, sp, kp, vp = _layer(xp, s0_p, None, None, wts)
        xs, ss, ksn, vsn = _layer(xs, state_gla[l], cache_sb_k[l], cache_sb_v[l], wts)
        for lst, val in zip(outs, (sp, kp, vp, ss, ksn, vsn)):
            lst.append(val)
    return (xp, xs, *(jnp.stack(o) for o in outs))
```

```python
import functools

import jax
import jax.numpy as jnp
from jax import lax
from jax.experimental import pallas as pl
from jax.experimental.pallas import tpu as pltpu

F32 = jnp.float32
BF16 = jnp.bfloat16

D_MODEL = 1024
CHUNK = 64
GLA_HEADS = 4
GLA_DK = 128
GLA_DV = 256
GLA_QK = GLA_HEADS * GLA_DK
GLA_V = GLA_HEADS * GLA_DV
GLA_GATE_RANK = 16
GLA_GATE_TEMP = 16.0
SB_HEADS = 8
SB_DH = 128
SB_W = SB_HEADS * SB_DH
D_FF = 2816
LN_EPS = 1e-5
DEPTH = 1
DN_ALPHA = (2 * DEPTH) ** 0.25

LOG2E = 1.4426950408889634
LANES = 128
GLA_ROWS = 512
GLA_BATCHED_ROWS = 256
ROW_TILE = 512
PROJ_ROW_TILE = 256
CAST_ROWS = 64
TAIL_ROW_TILE = 256
SB_BLOCK = 256
SB_QBLOCK = 256
SB_LOOKBACK = 1
SB_GROUP = 8
SB_DEAD = 110.0
VMEM_LIMIT = 56 * 1024 * 1024


def _layer_norm(r, g, b):
    mu = jnp.mean(r, axis=-1, keepdims=True)
    d = r - mu
    var = jnp.mean(d * d, axis=-1, keepdims=True)
    return d * lax.rsqrt(var + LN_EPS) * g + b


def _softplus_neg_abs(z):
    return jnp.log(1.0 + jnp.exp(-jnp.abs(z)))


def _split3(x):
    x1 = x.astype(BF16)
    r1 = x - x1.astype(F32)
    x2 = r1.astype(BF16)
    r2 = r1 - x2.astype(F32)
    return x1, x2, r2.astype(BF16)


def _dot(a, b):
    return jnp.dot(a, b, preferred_element_type=F32)


def _dot_nt(a, b):
    return lax.dot_general(a, b, (((1,), (1,)), ((), ())), preferred_element_type=F32)


def _dot_tn(a, b):
    return lax.dot_general(a, b, (((0,), (0,)), ((), ())), preferred_element_type=F32)


def _const_spec(shape):
    nd = len(shape)
    return pl.BlockSpec(shape, lambda *_: (0,) * nd, pipeline_mode=pl.Buffered(1))


def _params(*sem):
    return pltpu.CompilerParams(dimension_semantics=sem, vmem_limit_bytes=VMEM_LIMIT)


def _rowwise_call(body, row_inputs, consts, out_cols, tm, name):
    n_in, n_out, n_groups = len(row_inputs[0]), len(out_cols), len(row_inputs)
    rows = [g[0].shape[0] for g in row_inputs]
    tiles = [min(tm, r) for r in rows]
    steps = [r // t for r, t in zip(rows, tiles)]
    assert all(r % t == 0 for r, t in zip(rows, tiles))
    starts = [sum(steps[:g]) for g in range(n_groups)]

    def spec(g, width):
        return pl.BlockSpec((tiles[g], width), lambda i: (jnp.clip(i - starts[g], 0, steps[g] - 1), 0))

    def kern(*refs):
        i = pl.program_id(0)
        c0 = n_groups * n_in
        o0 = c0 + len(consts)
        for g in range(n_groups):
            @pl.when((i >= starts[g]) & (i < starts[g] + steps[g]))
            def _(g=g):
                body(*refs[g * n_in:(g + 1) * n_in], *refs[c0:o0], *refs[o0 + g * n_out:o0 + (g + 1) * n_out])

    outs = pl.pallas_call(
        kern,
        grid=(sum(steps),),
        in_specs=([spec(g, a.shape[1]) for g in range(n_groups) for a in row_inputs[g]]
                  + [_const_spec(c.shape) for c in consts]),
        out_specs=[spec(g, w) for g in range(n_groups) for w, _ in out_cols],
        out_shape=[jax.ShapeDtypeStruct((rows[g], w), d) for g in range(n_groups) for w, d in out_cols],
        compiler_params=_params("arbitrary"),
        name=name,
    )(*[a for g in row_inputs for a in g], *consts)
    return [outs[g * n_out:(g + 1) * n_out] for g in range(n_groups)]


def _swiglu_ln(x, wg_ref, wu_ref, wo_ref, g_ref, b_ref):
    xb = x.astype(BF16)
    gate = _dot(xb, wg_ref[...])
    up = _dot(xb, wu_ref[...])
    a = (gate * jax.nn.sigmoid(gate) * up).astype(BF16)
    return _layer_norm(DN_ALPHA * x + 0.5 * _dot(a, wo_ref[...]), g_ref[...], b_ref[...])


def _ffn_ln_kernel(x_ref, wg_ref, wu_ref, wo_ref, g_ref, b_ref, o_ref):
    o_ref[...] = _swiglu_ln(x_ref[...], wg_ref, wu_ref, wo_ref, g_ref, b_ref)


def _ffn_ln(xs, consts):
    outs = _rowwise_call(_ffn_ln_kernel, [[x] for x in xs], consts, [(D_MODEL, F32)], ROW_TILE, "ffn_ln")
    return [o[0] for o in outs]


_PROJ_GROUPS_A = (("gq", GLA_QK), ("gk", GLA_QK), ("gv", GLA_V), ("gr", GLA_V))
_PROJ_GROUPS_B = (("sq", SB_W), ("sk", SB_W), ("sv", SB_W), ("ga", D_MODEL), ("gb", D_MODEL))


def _proj_kernel(h_ref, wa_ref, wb_ref, wlr_ref, wup_ref, bup_ref,
                 gq_ref, gk_ref, gv_ref, gr_ref, la_ref, sq_ref, sk_ref, sv_ref, ga_ref, gb_ref,
                 skb_ref, svb_ref):
    hb = h_ref[...].astype(BF16)
    outs = dict(gq=gq_ref, gk=gk_ref, gv=gv_ref, gr=gr_ref, sq=sq_ref, sk=sk_ref, sv=sv_ref,
                ga=ga_ref, gb=gb_ref)
    for w_ref, groups in ((wa_ref, _PROJ_GROUPS_A), (wb_ref, _PROJ_GROUPS_B)):
        lo = 0
        for name, width in groups:
            y = _dot_nt(hb, w_ref[lo:lo + width, :])
            lo += width
            if name == "gq":
                y = y * (GLA_DK ** -0.5)
            if name == "sq":
                y = y * (SB_DH ** -0.5)
            if name in ("sq", "gv"):
                outs[name][...] = y.astype(BF16)
            else:
                outs[name][...] = y
            if name == "sk":
                skb_ref[...] = y.astype(BF16)
            if name == "sv":
                svb_ref[...] = y.astype(BF16)
    lr = _dot_nt(hb, wlr_ref[...]).astype(BF16)
    pre = _dot(lr, wup_ref[...]) + bup_ref[...]
    la_ref[...] = (jnp.minimum(pre, 0.0) - _softplus_neg_abs(pre)) * (1.0 / GLA_GATE_TEMP)


def _proj(hs, consts):
    widths = (GLA_QK, GLA_QK, GLA_V, GLA_V, GLA_QK, SB_W, SB_W, SB_W, D_MODEL, D_MODEL, SB_W, SB_W)
    dtypes = (F32, F32, BF16, F32, F32, BF16, F32, F32, F32, F32, BF16, BF16)
    return _rowwise_call(_proj_kernel, [[h] for h in hs], consts, list(zip(widths, dtypes)), PROJ_ROW_TILE, "proj")


def _gla_cumsum_matrix(tb):
    r = lax.broadcasted_iota(jnp.int32, (2 * tb, tb), 0)
    c = lax.broadcasted_iota(jnp.int32, (2 * tb, tb), 1)
    same_chunk = (r % tb) // CHUNK == c // CHUNK
    return (same_chunk & ((c <= r) | (r >= tb))).astype(BF16)


def _gla_kernel(q_ref, k_ref, v_ref, la_ref, gr_ref, gn_ref, s0_ref, cm_ref, o_ref, sout_ref, s_scr):
    t = pl.program_id(1)

    @pl.when(t == 0)
    def _():
        s_scr[...] = s0_ref[...]

    nb, tb, _ = q_ref.shape
    sb = cm_ref.shape[1]
    cm = cm_ref[...]
    gn = gn_ref[...]
    rows = lax.broadcasted_iota(jnp.int32, (sb, sb), 0)
    cols = lax.broadcasted_iota(jnp.int32, (sb, sb), 1)
    shift = CHUNK.bit_length() - 1
    causal = (cols <= rows) & ((rows >> shift) == (cols >> shift))

    for bi in range(nb):
        for r0 in range(0, tb, sb):
            rb = slice(r0, r0 + sb)
            for h in range(GLA_HEADS):
                ks = slice(h * GLA_DK, (h + 1) * GLA_DK)
                vs = slice(h * GLA_DV, (h + 1) * GLA_DV)
                cs = _dot(cm, jnp.concatenate(_split3(la_ref[bi, rb, ks]), axis=1))
                cs = cs[:, :GLA_DK] + cs[:, GLA_DK:2 * GLA_DK] + cs[:, 2 * GLA_DK:]
                b, b_tot = cs[:sb], cs[sb:]
                q = q_ref[bi, rb, ks]
                k = k_ref[bi, rb, ks]
                v = v_ref[bi, rb, vs]
                qg = (q * jnp.exp(b)).astype(BF16)
                kg = (k * jnp.exp(-b)).astype(BF16)
                kd = (k * jnp.exp(b_tot - b)).astype(BF16)
                att = jnp.where(causal, _dot_nt(qg, kg), 0.0).astype(BF16)
                o_intra = _dot(att, v)
                s = s_scr[bi, h]
                for c in range(sb // CHUNK):
                    rs = slice(c * CHUNK, (c + 1) * CHUNK)
                    ro = slice(r0 + c * CHUNK, r0 + (c + 1) * CHUNK)
                    o = o_intra[rs] + _dot(qg[rs], s.astype(BF16))
                    o = o * lax.rsqrt(jnp.mean(o * o, axis=-1, keepdims=True) + LN_EPS) * gn[:, vs]
                    gate = gr_ref[bi, ro, vs]
                    o_ref[bi, ro, vs] = (o * (gate * jax.nn.sigmoid(gate))).astype(BF16)
                    decay = jnp.exp(jnp.broadcast_to(b_tot[c * CHUNK:c * CHUNK + 1], (GLA_DK, GLA_DK))).T
                    s = jnp.concatenate([decay, decay], axis=1) * s + _dot_tn(kd[rs], v[rs])
                s_scr[bi, h] = s

    @pl.when(t == pl.num_programs(1) - 1)
    def _():
        sout_ref[...] = s_scr[...]


def _gla(q, k, v, la, gr, gn, s0):
    bsz, t, _ = q.shape
    tb = min(GLA_ROWS, t)
    nb = max(1, min(bsz, GLA_ROWS // t))
    assert t % tb == 0 and bsz % nb == 0 and tb % CHUNK == 0
    cm = _gla_cumsum_matrix(min(tb, GLA_BATCHED_ROWS))
    qk_spec = pl.BlockSpec((nb, tb, GLA_QK), lambda b, i: (b, i, 0))
    v_spec = pl.BlockSpec((nb, tb, GLA_V), lambda b, i: (b, i, 0))
    s_spec = pl.BlockSpec((nb, GLA_HEADS, GLA_DK, GLA_DV), lambda b, i: (b, 0, 0, 0))
    return pl.pallas_call(
        _gla_kernel,
        grid=(bsz // nb, t // tb),
        in_specs=[qk_spec, qk_spec, v_spec, qk_spec, v_spec, _const_spec(gn.shape), s_spec,
                  _const_spec(cm.shape)],
        out_specs=[v_spec, s_spec],
        out_shape=[jax.ShapeDtypeStruct((bsz, t, GLA_V), BF16),
                   jax.ShapeDtypeStruct((bsz, GLA_HEADS, GLA_DK, GLA_DV), F32)],
        scratch_shapes=[pltpu.VMEM((nb, GLA_HEADS, GLA_DK, GLA_DV), F32)],
        compiler_params=_params("parallel", "arbitrary"),
        name="gla",
    )(q, k, v, la, gr, gn, s0, cm)


def _suffix_matrix(tk):
    r = lax.broadcasted_iota(jnp.int32, (2 * tk, tk), 0) % tk
    c = lax.broadcasted_iota(jnp.int32, (2 * tk, tk), 1)
    return (r >= c).astype(BF16)


def _sb_update(qs, ks, vs, carry_ref, acc_ref, suffix, visible, first):
    heads = range(len(qs))
    sub = suffix.shape[1]
    n_sub = ks[0].shape[0] // sub
    z = [_dot_nt(qs[hd], ks[hd]) for hd in heads]
    carry = [None if first else carry_ref[hd] for hd in heads]
    ws = [[None] * n_sub for _ in heads]
    for s in reversed(range(n_sub)):
        vis = None if visible is None else visible[s]
        zs = [z[hd][:, s * sub:(s + 1) * sub] for hd in heads]
        if vis is not None:
            zs = [jnp.where(vis, x, -jnp.inf) for x in zs]
        cs = [jnp.maximum(x, 0.0) + jnp.log(1.0 + jnp.exp2(jnp.abs(x) * -LOG2E)) for x in zs]
        his = [c.astype(BF16) for c in cs]
        los = [(c - hi.astype(F32)).astype(BF16) for c, hi in zip(cs, his)]
        sums = [_dot(jnp.concatenate([hi, lo], axis=1), suffix) for hi, lo in zip(his, los)]
        for hd in heads:
            e = zs[hd] - sums[hd]
            if carry[hd] is not None:
                e = e - jnp.concatenate([carry[hd]] * (sub // LANES), axis=1)
            ws[hd][s] = jnp.exp(e).astype(BF16)
            total = jnp.broadcast_to(jnp.sum(cs[hd], axis=-1, keepdims=True), (cs[hd].shape[0], LANES))
            carry[hd] = total if carry[hd] is None else carry[hd] + total
    for hd in heads:
        pv = _dot(ws[hd][0] if n_sub == 1 else jnp.concatenate(ws[hd], axis=1), vs[hd])
        acc_ref[hd] = pv if first else acc_ref[hd] + pv
        carry_ref[hd] = carry[hd]


def _head(hd):
    return slice(hd * SB_DH, (hd + 1) * SB_DH)


def _sb_prompt_kernel(q_ref, k_ref, v_ref, sfx_ref, o_ref, carry_scr, acc_scr):
    i = pl.program_id(2)
    tq = q_ref.shape[1]
    heads = q_ref.shape[2] // SB_DH
    sfx = sfx_ref[...]
    rows = lax.broadcasted_iota(jnp.int32, (tq, tq), 0)
    cols = lax.broadcasted_iota(jnp.int32, (tq, tq), 1)
    strictly_earlier = cols < rows
    qs = [q_ref[0, :, _head(hd)] for hd in range(heads)]

    def keys(ref, start, n):
        return [ref[0, pl.ds(start, n), _head(hd)] for hd in range(heads)]

    for n_prev in range(SB_LOOKBACK + 1):
        @pl.when((i == n_prev) if n_prev < SB_LOOKBACK else (i >= n_prev))
        def _(n_prev=n_prev):
            p0 = pl.multiple_of((i - n_prev) * tq, tq)
            n = (n_prev + 1) * tq
            _sb_update(qs, keys(k_ref, p0, n), keys(v_ref, p0, n), carry_scr, acc_scr, sfx,
                       (None,) * n_prev + (strictly_earlier,), True)

    n_older = jnp.maximum(i - SB_LOOKBACK, 0)

    def cond(state):
        j, cmin = state
        return (j < n_older) & (cmin <= SB_DEAD)

    def body(state):
        j, _ = state
        k0 = pl.multiple_of((i - SB_LOOKBACK - 1 - j) * tq, tq)
        _sb_update(qs, keys(k_ref, k0, tq), keys(v_ref, k0, tq), carry_scr, acc_scr, sfx, None, False)
        return j + 1, jnp.min(carry_scr[...])

    lax.while_loop(cond, body, (jnp.int32(0), jnp.min(carry_scr[...])))
    for hd in range(heads):
        o_ref[0, :, _head(hd)] = acc_scr[hd].astype(o_ref.dtype)


def _sb_prompt(sq, sk, sv):
    bsz, t, _ = sq.shape
    blk = SB_QBLOCK
    assert t % blk == 0
    width = SB_GROUP * SB_DH
    sfx = _suffix_matrix(blk)
    q_spec = pl.BlockSpec((1, blk, width), lambda b, g, i: (b, i, g))
    kv_spec = pl.BlockSpec((1, t, width), lambda b, g, i: (b, 0, g), pipeline_mode=pl.Buffered(1))
    return pl.pallas_call(
        _sb_prompt_kernel,
        grid=(bsz, SB_HEADS // SB_GROUP, t // blk),
        in_specs=[q_spec, kv_spec, kv_spec, _const_spec(sfx.shape)],
        out_specs=q_spec,
        out_shape=jax.ShapeDtypeStruct((bsz, t, SB_W), BF16),
        scratch_shapes=[pltpu.VMEM((SB_GROUP, blk, LANES), F32), pltpu.VMEM((SB_GROUP, blk, SB_DH), F32)],
        compiler_params=_params("parallel", "parallel", "arbitrary"),
        name="sb_prompt",
    )(sq, sk, sv, sfx)


def _sb_sample_kernel(q_ref, kn_ref, vn_ref, kw_ref, vw_ref, k_hbm, v_hbm, sfx_ref, o_ref,
                      carry_scr, acc_scr, kbuf, vbuf, sem, *, n_past):
    b = pl.program_id(0)
    tq = q_ref.shape[1]
    blk = SB_BLOCK
    win = blk - tq
    sfx = sfx_ref[...]
    rows = lax.broadcasted_iota(jnp.int32, (tq, blk), 0)
    cols = lax.broadcasted_iota(jnp.int32, (tq, blk), 1)

    def head_rows(ref, hd, first, n):
        return ref[pl.ds(first * SB_HEADS + hd, n, stride=SB_HEADS), :].astype(BF16)

    visible = (cols < win) | (cols - win < rows)
    qs = [q_ref[0, :, _head(hd)] for hd in range(SB_HEADS)]
    _sb_update(qs,
               [jnp.concatenate([head_rows(kw_ref.at[0], hd, blk - win, win), kn_ref[0, :, _head(hd)]], axis=0)
                for hd in range(SB_HEADS)],
               [jnp.concatenate([head_rows(vw_ref.at[0], hd, blk - win, win), vn_ref[0, :, _head(hd)]], axis=0)
                for hd in range(SB_HEADS)],
               carry_scr, acc_scr, sfx, (visible,), True)

    n_left = n_past - win
    n_older = -(-n_left // blk)

    def cond(state):
        j, cmin = state
        return (j < n_older) & (cmin <= SB_DEAD)

    def body(state):
        j, _ = state
        end = n_left - j * blk
        start = jnp.maximum(end - blk, 0)
        row0 = pl.multiple_of(start * SB_HEADS, SB_HEADS)
        copy_k = pltpu.make_async_copy(k_hbm.at[b, pl.ds(row0, blk * SB_HEADS)], kbuf, sem.at[0])
        copy_v = pltpu.make_async_copy(v_hbm.at[b, pl.ds(row0, blk * SB_HEADS)], vbuf, sem.at[1])
        copy_k.start()
        copy_v.start()
        copy_k.wait()
        copy_v.wait()
        unread = cols < end - start
        _sb_update(qs, [head_rows(kbuf, hd, 0, blk) for hd in range(SB_HEADS)],
                   [head_rows(vbuf, hd, 0, blk) for hd in range(SB_HEADS)],
                   carry_scr, acc_scr, sfx, (unread,), False)
        return j + 1, jnp.min(carry_scr[...])

    lax.while_loop(cond, body, (jnp.int32(0), jnp.min(carry_scr[...])))
    for hd in range(SB_HEADS):
        o_ref[0, :, _head(hd)] = acc_scr[hd].astype(o_ref.dtype)


def _sb_sample(sq, sk, sv, k_past, v_past):
    bsz, t, _ = sq.shape
    p = k_past.shape[1]
    blk = SB_BLOCK
    assert p % blk == 0 and p >= blk and t < blk and t % 16 == 0
    k_past = k_past.reshape(bsz, p * SB_HEADS, SB_DH)
    v_past = v_past.reshape(bsz, p * SB_HEADS, SB_DH)
    sfx = _suffix_matrix(blk)
    new_spec = pl.BlockSpec((1, t, SB_W), lambda b: (b, 0, 0))
    window_spec = pl.BlockSpec((1, blk * SB_HEADS, SB_DH), lambda b: (b, p // blk - 1, 0))
    hbm_spec = pl.BlockSpec(memory_space=pl.ANY)
    return pl.pallas_call(
        functools.partial(_sb_sample_kernel, n_past=p),
        grid=(bsz,),
        in_specs=[new_spec, new_spec, new_spec, window_spec, window_spec, hbm_spec, hbm_spec,
                  _const_spec(sfx.shape)],
        out_specs=new_spec,
        out_shape=jax.ShapeDtypeStruct((bsz, t, SB_W), BF16),
        scratch_shapes=[pltpu.VMEM((SB_HEADS, t, LANES), F32), pltpu.VMEM((SB_HEADS, t, SB_DH), F32),
                        pltpu.VMEM((blk * SB_HEADS, SB_DH), F32), pltpu.VMEM((blk * SB_HEADS, SB_DH), F32),
                        pltpu.SemaphoreType.DMA((2,))],
        compiler_params=_params("arbitrary"),
        name="sb_sample",
    )(sq, sk, sv, k_past, v_past, k_past, v_past, sfx)


def _merge_ffn_kernel(h_ref, oa_ref, ob_ref, ga_ref, gb_ref, wa_ref, wb_ref, wm_ref, g2_ref, b2_ref,
                      wg_ref, wu_ref, wo_ref, g3_ref, b3_ref, o_ref):
    branch_a = _dot(oa_ref[...], wa_ref[...])
    branch_b = _dot(ob_ref[...], wb_ref[...])
    merged = jax.nn.sigmoid(ga_ref[...]) * branch_a + jax.nn.sigmoid(gb_ref[...]) * branch_b
    mix = _dot(merged.astype(BF16), wm_ref[...])
    h2 = _layer_norm(DN_ALPHA * h_ref[...] + mix, g2_ref[...], b2_ref[...])
    o_ref[...] = _swiglu_ln(h2, wg_ref, wu_ref, wo_ref, g3_ref, b3_ref)


def _merge_ffn(groups, consts):
    outs = _rowwise_call(_merge_ffn_kernel, groups, consts, [(D_MODEL, F32)], TAIL_ROW_TILE, "merge_ffn")
    return [o[0] for o in outs]


def _cast_kernel(*refs, plans):
    n_in = len(plans)
    outs = iter(refs[n_in:])
    for ref, plan in zip(refs[:n_in], plans):
        x = ref[...]
        for lo, hi, keep in plan:
            y = x[:, lo:hi]
            if keep < hi - lo:
                y = jnp.where(lax.broadcasted_iota(jnp.int32, y.shape, 1) < keep, y, 0.0)
            next(outs)[...] = y.astype(BF16)


def _cast_weights(arrays, plans, rows_per_step):
    rows = arrays[0].shape[0]
    assert all(a.shape[0] == rows for a in arrays) and rows % rows_per_step == 0
    out_shape = [jax.ShapeDtypeStruct((rows, hi - lo), BF16) for plan in plans for lo, hi, _ in plan]
    return pl.pallas_call(
        functools.partial(_cast_kernel, plans=plans),
        grid=(rows // rows_per_step,),
        in_specs=[pl.BlockSpec((rows_per_step, a.shape[1]), lambda i: (i, 0)) for a in arrays],
        out_specs=[pl.BlockSpec((rows_per_step, o.shape[1]), lambda i: (i, 0)) for o in out_shape],
        out_shape=out_shape,
        compiler_params=_params("parallel"),
        name="cast_weights",
    )(*arrays)


def _row(v):
    return v.reshape(1, -1)


def _layer(xs, s0s, pasts, wts):
    shapes = [x.shape[:2] for x in xs]
    hs = _ffn_ln([x.reshape(-1, D_MODEL) for x in xs], wts["ffn1"])
    projected = _proj(hs, wts["proj"])
    tails, states, new_kv = [], [], []
    for (bsz, t), h, s0, past, outs in zip(shapes, hs, s0s, pasts, projected):
        gq, gk, gv, gr, la, sq, sk, sv, ga, gb, skb, svb = outs

        def seq(a):
            return a.reshape(bsz, t, a.shape[-1])

        o_a, s_new = _gla(seq(gq), seq(gk), seq(gv), seq(la), seq(gr), wts["gla_norm"], s0)
        if past is None:
            o_b = _sb_prompt(seq(sq), seq(skb), seq(svb))
        else:
            o_b = _sb_sample(seq(sq), seq(skb), seq(svb), *past)
        tails.append([h, o_a.reshape(-1, GLA_V), o_b.reshape(-1, SB_W), ga, gb])
        states.append(s_new)
        new_kv.append((sk.reshape(bsz, t, SB_HEADS, SB_DH), sv.reshape(bsz, t, SB_HEADS, SB_DH)))
    ys = _merge_ffn(tails, wts["merge"] + wts["ffn2"])
    return [(y.reshape(bsz, t, D_MODEL), s_new, k_new, v_new)
            for (bsz, t), y, s_new, (k_new, v_new) in zip(shapes, ys, states, new_kv)]


def kernel(x_prompt, x_sample, state_gla, cache_sb_k, cache_sb_v, ffn1_w_in, ffn1_w_out, ln1_g, ln1_b, w_in, w_gla_gate_up, b_gla_gate, g_gla_norm, w_gla_o, w_sb_o, w_out, ln2_g, ln2_b, ffn2_w_in, ffn2_w_out, ln3_g, ln3_b):
    xp, xs = x_prompt, x_sample
    bp = xp.shape[0]
    outs = [[] for _ in range(6)]
    lr_lo = 2 * GLA_QK + 2 * GLA_V
    lr_hi = lr_lo + GLA_GATE_RANK
    whole = ((0, D_MODEL, D_MODEL),)
    halves = ((0, D_FF, D_FF), (D_FF, 2 * D_FF, D_FF))
    for l in range(DEPTH):
        wg1, wu1, wg2, wu2, w_ga, w_sb, w_o = _cast_weights(
            [ffn1_w_in[l], ffn2_w_in[l], w_gla_o[l], w_sb_o[l], w_out[l]],
            (halves, halves, whole, whole, whole), CAST_ROWS)
        w_in_t = jnp.swapaxes(w_in[l], 0, 1)
        w_a = w_in_t[:lr_lo].astype(BF16)
        w_lr = jnp.pad(w_in_t[lr_lo:lr_hi], ((0, LANES - GLA_GATE_RANK), (0, 0))).astype(BF16)
        w_b = w_in_t[lr_hi:].astype(BF16)
        wo1, wo2 = _cast_weights([ffn1_w_out[l], ffn2_w_out[l]], (whole, whole), D_FF // 4)
        w_up = jnp.pad(w_gla_gate_up[l], ((0, LANES - GLA_GATE_RANK), (0, 0))).astype(BF16)
        wts = {
            "ffn1": (wg1, wu1, wo1, _row(ln1_g[l]), _row(ln1_b[l])),
            "proj": (w_a, w_b, w_lr, w_up, _row(b_gla_gate[l])),
            "gla_norm": _row(g_gla_norm[l]),
            "merge": (w_ga, w_sb, w_o, _row(ln2_g[l]), _row(ln2_b[l])),
            "ffn2": (wg2, wu2, wo2, _row(ln3_g[l]), _row(ln3_b[l])),
        }
        s0_p = jnp.zeros((bp, GLA_HEADS, GLA_DK, GLA_DV), F32)
        (xp, sp, kp, vp), (xs, ss, ksn, vsn) = _layer(
            [xp, xs], [s0_p, state_gla[l]], [None, (cache_sb_k[l], cache_sb_v[l])], wts)
        for lst, val in zip(outs, (sp, kp, vp, ss, ksn, vsn)):
            lst.append(val)
    return (xp, xs, *(jnp.stack(o) for o in outs))
```

```python
import functools

import jax
import jax.numpy as jnp
from jax import lax
from jax.experimental import pallas as pl
from jax.experimental.pallas import tpu as pltpu

F32 = jnp.float32
BF16 = jnp.bfloat16

D_MODEL = 1024
CHUNK = 64
GLA_HEADS = 4
GLA_DK = 128
GLA_DV = 256
GLA_QK = GLA_HEADS * GLA_DK
GLA_V = GLA_HEADS * GLA_DV
GLA_GATE_RANK = 16
GLA_GATE_TEMP = 16.0
SB_HEADS = 8
SB_DH = 128
SB_W = SB_HEADS * SB_DH
D_FF = 2816
LN_EPS = 1e-5
DEPTH = 1
DN_ALPHA = (2 * DEPTH) ** 0.25

LOG2E = 1.4426950408889634
LANES = 128
GLA_ROWS = 512
GLA_BATCHED_ROWS = 256
ROW_TILE = 512
PROJ_ROW_TILE = 256
CAST_ROWS = 64
TAIL_ROW_TILE = 256
SB_BLOCK = 256
SB_QBLOCK = 256
SB_LOOKBACK = 1
SB_GROUP = 8
SB_DEAD = 110.0
VMEM_LIMIT = 56 * 1024 * 1024


def _layer_norm(r, g, b):
    mu = jnp.mean(r, axis=-1, keepdims=True)
    d = r - mu
    var = jnp.mean(d * d, axis=-1, keepdims=True)
    return d * lax.rsqrt(var + LN_EPS) * g + b


def _softplus_neg_abs(z):
    return jnp.log(1.0 + jnp.exp(-jnp.abs(z)))


def _split_hi_lo(x):
    hi = x.astype(BF16)
    return hi, (x - hi.astype(F32)).astype(BF16)


def _dot(a, b):
    return jnp.dot(a, b, preferred_element_type=F32)


def _dot_nt(a, b):
    return lax.dot_general(a, b, (((1,), (1,)), ((), ())), preferred_element_type=F32)


def _dot_tn(a, b):
    return lax.dot_general(a, b, (((0,), (0,)), ((), ())), preferred_element_type=F32)


def _const_spec(shape):
    nd = len(shape)
    return pl.BlockSpec(shape, lambda *_: (0,) * nd, pipeline_mode=pl.Buffered(1))


def _params(*sem):
    return pltpu.CompilerParams(dimension_semantics=sem, vmem_limit_bytes=VMEM_LIMIT)


def _rowwise_call(body, row_inputs, consts, out_cols, tm, name):
    n_in, n_out, n_groups = len(row_inputs[0]), len(out_cols), len(row_inputs)
    rows = [g[0].shape[0] for g in row_inputs]
    tiles = [min(tm, r) for r in rows]
    steps = [r // t for r, t in zip(rows, tiles)]
    assert all(r % t == 0 for r, t in zip(rows, tiles))
    starts = [sum(steps[:g]) for g in range(n_groups)]

    def spec(g, width):
        return pl.BlockSpec((tiles[g], width), lambda i: (jnp.clip(i - starts[g], 0, steps[g] - 1), 0))

    def kern(*refs):
        i = pl.program_id(0)
        c0 = n_groups * n_in
        o0 = c0 + len(consts)
        for g in range(n_groups):
            @pl.when((i >= starts[g]) & (i < starts[g] + steps[g]))
            def _(g=g):
                body(*refs[g * n_in:(g + 1) * n_in], *refs[c0:o0], *refs[o0 + g * n_out:o0 + (g + 1) * n_out])

    outs = pl.pallas_call(
        kern,
        grid=(sum(steps),),
        in_specs=([spec(g, a.shape[1]) for g in range(n_groups) for a in row_inputs[g]]
                  + [_const_spec(c.shape) for c in consts]),
        out_specs=[spec(g, w) for g in range(n_groups) for w, _ in out_cols],
        out_shape=[jax.ShapeDtypeStruct((rows[g], w), d) for g in range(n_groups) for w, d in out_cols],
        compiler_params=_params("arbitrary"),
        name=name,
    )(*[a for g in row_inputs for a in g], *consts)
    return [outs[g * n_out:(g + 1) * n_out] for g in range(n_groups)]


def _swiglu_ln(x, wg_ref, wu_ref, wo_ref, g_ref, b_ref):
    xb = x.astype(BF16)
    gate = _dot(xb, wg_ref[...])
    up = _dot(xb, wu_ref[...])
    a = (gate * jax.nn.sigmoid(gate) * up).astype(BF16)
    return _layer_norm(DN_ALPHA * x + 0.5 * _dot(a, wo_ref[...]), g_ref[...], b_ref[...])


def _ffn_ln_kernel(x_ref, wg_ref, wu_ref, wo_ref, g_ref, b_ref, o_ref):
    o_ref[...] = _swiglu_ln(x_ref[...], wg_ref, wu_ref, wo_ref, g_ref, b_ref)


def _ffn_ln(xs, consts):
    outs = _rowwise_call(_ffn_ln_kernel, [[x] for x in xs], consts, [(D_MODEL, F32)], ROW_TILE, "ffn_ln")
    return [o[0] for o in outs]


_PROJ_GROUPS_A = (("gq", GLA_QK), ("gk", GLA_QK), ("gv", GLA_V), ("gr", GLA_V))
_PROJ_GROUPS_B = (("sq", SB_W), ("sk", SB_W), ("sv", SB_W), ("ga", D_MODEL), ("gb", D_MODEL))


def _proj_kernel(h_ref, wa_ref, wb_ref, wlr_ref, wup_ref, bup_ref,
                 gq_ref, gk_ref, gv_ref, gr_ref, la_ref, sq_ref, sk_ref, sv_ref, ga_ref, gb_ref,
                 skb_ref, svb_ref):
    hb = h_ref[...].astype(BF16)
    outs = dict(gq=gq_ref, gk=gk_ref, gv=gv_ref, gr=gr_ref, sq=sq_ref, sk=sk_ref, sv=sv_ref,
                ga=ga_ref, gb=gb_ref)
    for w_ref, groups in ((wa_ref, _PROJ_GROUPS_A), (wb_ref, _PROJ_GROUPS_B)):
        lo = 0
        for name, width in groups:
            y = _dot_nt(hb, w_ref[lo:lo + width, :])
            lo += width
            if name == "gq":
                y = y * (GLA_DK ** -0.5)
            if name == "sq":
                y = y * (SB_DH ** -0.5)
            if name in ("sq", "gv"):
                outs[name][...] = y.astype(BF16)
            else:
                outs[name][...] = y
            if name == "sk":
                skb_ref[...] = y.astype(BF16)
            if name == "sv":
                svb_ref[...] = y.astype(BF16)
    lr = _dot_nt(hb, wlr_ref[...]).astype(BF16)
    pre = _dot(lr, wup_ref[...]) + bup_ref[...]
    la_ref[...] = (jnp.minimum(pre, 0.0) - _softplus_neg_abs(pre)) * (1.0 / GLA_GATE_TEMP)


def _proj(hs, consts):
    widths = (GLA_QK, GLA_QK, GLA_V, GLA_V, GLA_QK, SB_W, SB_W, SB_W, D_MODEL, D_MODEL, SB_W, SB_W)
    dtypes = (F32, F32, BF16, F32, F32, BF16, F32, F32, F32, F32, BF16, BF16)
    return _rowwise_call(_proj_kernel, [[h] for h in hs], consts, list(zip(widths, dtypes)), PROJ_ROW_TILE, "proj")


def _gla_cumsum_matrix(tb):
    r = lax.broadcasted_iota(jnp.int32, (2 * tb, tb), 0)
    c = lax.broadcasted_iota(jnp.int32, (2 * tb, tb), 1)
    same_chunk = (r % tb) // CHUNK == c // CHUNK
    return (same_chunk & ((c <= r) | (r >= tb))).astype(BF16)


def _gla_kernel(q_ref, k_ref, v_ref, la_ref, gr_ref, gn_ref, s0_ref, cm_ref, o_ref, sout_ref, s_scr):
    t = pl.program_id(1)

    @pl.when(t == 0)
    def _():
        s_scr[...] = s0_ref[...]

    nb, tb, _ = q_ref.shape
    sb = cm_ref.shape[1]
    cm = cm_ref[...]
    gn = gn_ref[...]
    rows = lax.broadcasted_iota(jnp.int32, (sb, sb), 0)
    cols = lax.broadcasted_iota(jnp.int32, (sb, sb), 1)
    shift = CHUNK.bit_length() - 1
    causal = (cols <= rows) & ((rows >> shift) == (cols >> shift))

    for bi in range(nb):
        for r0 in range(0, tb, sb):
            rb = slice(r0, r0 + sb)
            for h in range(GLA_HEADS):
                ks = slice(h * GLA_DK, (h + 1) * GLA_DK)
                vs = slice(h * GLA_DV, (h + 1) * GLA_DV)
                cs = _dot(cm, jnp.concatenate(_split_hi_lo(la_ref[bi, rb, ks]), axis=1))
                cs = cs[:, :GLA_DK] + cs[:, GLA_DK:]
                b, b_tot = cs[:sb], cs[sb:]
                q = q_ref[bi, rb, ks]
                k = k_ref[bi, rb, ks]
                v = v_ref[bi, rb, vs]
                qg = (q * jnp.exp(b)).astype(BF16)
                kg = (k * jnp.exp(-b)).astype(BF16)
                kd = (k * jnp.exp(b_tot - b)).astype(BF16)
                att = jnp.where(causal, _dot_nt(qg, kg), 0.0).astype(BF16)
                o_intra = _dot(att, v)
                s = s_scr[bi, h]
                for c in range(sb // CHUNK):
                    rs = slice(c * CHUNK, (c + 1) * CHUNK)
                    ro = slice(r0 + c * CHUNK, r0 + (c + 1) * CHUNK)
                    o = o_intra[rs] + _dot(qg[rs], s.astype(BF16))
                    o = o * lax.rsqrt(jnp.mean(o * o, axis=-1, keepdims=True) + LN_EPS) * gn[:, vs]
                    gate = gr_ref[bi, ro, vs]
                    o_ref[bi, ro, vs] = (o * (gate * jax.nn.sigmoid(gate))).astype(BF16)
                    decay = jnp.exp(jnp.broadcast_to(b_tot[c * CHUNK:c * CHUNK + 1], (GLA_DK, GLA_DK))).T
                    s = jnp.concatenate([decay, decay], axis=1) * s + _dot_tn(kd[rs], v[rs])
                s_scr[bi, h] = s

    @pl.when(t == pl.num_programs(1) - 1)
    def _():
        sout_ref[...] = s_scr[...]


def _gla(q, k, v, la, gr, gn, s0):
    bsz, t, _ = q.shape
    tb = min(GLA_ROWS, t)
    nb = max(1, min(bsz, GLA_ROWS // t))
    assert t % tb == 0 and bsz % nb == 0 and tb % CHUNK == 0
    cm = _gla_cumsum_matrix(min(tb, GLA_BATCHED_ROWS))
    qk_spec = pl.BlockSpec((nb, tb, GLA_QK), lambda b, i: (b, i, 0))
    v_spec = pl.BlockSpec((nb, tb, GLA_V), lambda b, i: (b, i, 0))
    s_spec = pl.BlockSpec((nb, GLA_HEADS, GLA_DK, GLA_DV), lambda b, i: (b, 0, 0, 0))
    return pl.pallas_call(
        _gla_kernel,
        grid=(bsz // nb, t // tb),
        in_specs=[qk_spec, qk_spec, v_spec, qk_spec, v_spec, _const_spec(gn.shape), s_spec,
                  _const_spec(cm.shape)],
        out_specs=[v_spec, s_spec],
        out_shape=[jax.ShapeDtypeStruct((bsz, t, GLA_V), BF16),
                   jax.ShapeDtypeStruct((bsz, GLA_HEADS, GLA_DK, GLA_DV), F32)],
        scratch_shapes=[pltpu.VMEM((nb, GLA_HEADS, GLA_DK, GLA_DV), F32)],
        compiler_params=_params("parallel", "arbitrary"),
        name="gla",
    )(q, k, v, la, gr, gn, s0, cm)


def _suffix_matrix(tk):
    r = lax.broadcasted_iota(jnp.int32, (2 * tk, tk), 0) % tk
    c = lax.broadcasted_iota(jnp.int32, (2 * tk, tk), 1)
    return (r >= c).astype(BF16)


def _sb_update(qs, ks, vs, carry_ref, acc_ref, suffix, visible, first, cmin_ref=None):
    heads = range(len(qs))
    sub = suffix.shape[1]
    n_sub = ks[0].shape[0] // sub
    z = [_dot_nt(qs[hd], ks[hd]) for hd in heads]
    carry = [None if first else carry_ref[hd] for hd in heads]
    ws = [[None] * n_sub for _ in heads]
    for s in reversed(range(n_sub)):
        vis = None if visible is None else visible[s]
        zs = [z[hd][:, s * sub:(s + 1) * sub] for hd in heads]
        if vis is not None:
            zs = [jnp.where(vis, x, -jnp.inf) for x in zs]
        cs = [jnp.maximum(x, 0.0) + jnp.log(1.0 + jnp.exp2(jnp.abs(x) * -LOG2E)) for x in zs]
        sums = [_dot(jnp.concatenate(_split_hi_lo(c), axis=1), suffix) for c in cs]
        for hd in heads:
            e = zs[hd] - sums[hd]
            if carry[hd] is not None:
                e = e - jnp.concatenate([carry[hd]] * (sub // LANES), axis=1)
            ws[hd][s] = jnp.exp(e).astype(BF16)
            total = jnp.broadcast_to(jnp.sum(cs[hd], axis=-1, keepdims=True), (cs[hd].shape[0], LANES))
            carry[hd] = total if carry[hd] is None else carry[hd] + total
    if cmin_ref is not None:
        cmin_ref[0] = jnp.min(functools.reduce(jnp.minimum, carry))
    for hd in heads:
        pv = _dot(ws[hd][0] if n_sub == 1 else jnp.concatenate(ws[hd], axis=1), vs[hd])
        acc_ref[hd] = pv if first else acc_ref[hd] + pv
        carry_ref[hd] = carry[hd]


def _head(hd):
    return slice(hd * SB_DH, (hd + 1) * SB_DH)


def _sb_prompt_kernel(q_ref, k_ref, v_ref, sfx_ref, o_ref, carry_scr, acc_scr, cmin_scr):
    i = pl.program_id(2)
    tq = q_ref.shape[1]
    heads = q_ref.shape[2] // SB_DH
    sfx = sfx_ref[...]
    rows = lax.broadcasted_iota(jnp.int32, (tq, tq), 0)
    cols = lax.broadcasted_iota(jnp.int32, (tq, tq), 1)
    strictly_earlier = cols < rows
    qs = [q_ref[0, :, _head(hd)] for hd in range(heads)]

    def keys(ref, start, n):
        return [ref[0, pl.ds(start, n), _head(hd)] for hd in range(heads)]

    for n_prev in range(SB_LOOKBACK + 1):
        @pl.when((i == n_prev) if n_prev < SB_LOOKBACK else (i >= n_prev))
        def _(n_prev=n_prev):
            p0 = pl.multiple_of((i - n_prev) * tq, tq)
            n = (n_prev + 1) * tq
            _sb_update(qs, keys(k_ref, p0, n), keys(v_ref, p0, n), carry_scr, acc_scr, sfx,
                       (None,) * n_prev + (strictly_earlier,), True, cmin_scr)

    n_older = jnp.maximum(i - SB_LOOKBACK, 0)

    def cond(state):
        j, cmin = state
        return (j < n_older) & (cmin <= SB_DEAD)

    def body(state):
        j, _ = state
        k0 = pl.multiple_of((i - SB_LOOKBACK - 1 - j) * tq, tq)
        _sb_update(qs, keys(k_ref, k0, tq), keys(v_ref, k0, tq), carry_scr, acc_scr, sfx, None, False, cmin_scr)
        return j + 1, cmin_scr[0]

    lax.while_loop(cond, body, (jnp.int32(0), cmin_scr[0]))
    for hd in range(heads):
        o_ref[0, :, _head(hd)] = acc_scr[hd].astype(o_ref.dtype)


def _sb_prompt(sq, sk, sv):
    bsz, t, _ = sq.shape
    blk = SB_QBLOCK
    assert t % blk == 0
    width = SB_GROUP * SB_DH
    sfx = _suffix_matrix(blk)
    q_spec = pl.BlockSpec((1, blk, width), lambda b, g, i: (b, i, g))
    kv_spec = pl.BlockSpec((1, t, width), lambda b, g, i: (b, 0, g), pipeline_mode=pl.Buffered(1))
    return pl.pallas_call(
        _sb_prompt_kernel,
        grid=(bsz, SB_HEADS // SB_GROUP, t // blk),
        in_specs=[q_spec, kv_spec, kv_spec, _const_spec(sfx.shape)],
        out_specs=q_spec,
        out_shape=jax.ShapeDtypeStruct((bsz, t, SB_W), BF16),
        scratch_shapes=[pltpu.VMEM((SB_GROUP, blk, LANES), F32), pltpu.VMEM((SB_GROUP, blk, SB_DH), F32),
                        pltpu.SMEM((1,), F32)],
        compiler_params=_params("parallel", "parallel", "arbitrary"),
        name="sb_prompt",
    )(sq, sk, sv, sfx)


def _sb_sample_kernel(q_ref, kn_ref, vn_ref, kw_ref, vw_ref, k_hbm, v_hbm, sfx_ref, o_ref,
                      carry_scr, acc_scr, kbuf, vbuf, sem, cmin_scr, *, n_past):
    b = pl.program_id(0)
    tq = q_ref.shape[1]
    blk = SB_BLOCK
    win = blk - tq
    sfx = sfx_ref[...]
    rows = lax.broadcasted_iota(jnp.int32, (tq, blk), 0)
    cols = lax.broadcasted_iota(jnp.int32, (tq, blk), 1)

    def head_rows(ref, hd, first, n):
        return ref[pl.ds(first * SB_HEADS + hd, n, stride=SB_HEADS), :].astype(BF16)

    visible = (cols < win) | (cols - win < rows)
    qs = [q_ref[0, :, _head(hd)] for hd in range(SB_HEADS)]
    _sb_update(qs,
               [jnp.concatenate([head_rows(kw_ref.at[0], hd, blk - win, win), kn_ref[0, :, _head(hd)]], axis=0)
                for hd in range(SB_HEADS)],
               [jnp.concatenate([head_rows(vw_ref.at[0], hd, blk - win, win), vn_ref[0, :, _head(hd)]], axis=0)
                for hd in range(SB_HEADS)],
               carry_scr, acc_scr, sfx, (visible,), True, cmin_scr)

    n_left = n_past - win
    n_older = -(-n_left // blk)

    def cond(state):
        j, cmin = state
        return (j < n_older) & (cmin <= SB_DEAD)

    def body(state):
        j, _ = state
        end = n_left - j * blk
        start = jnp.maximum(end - blk, 0)
        row0 = pl.multiple_of(start * SB_HEADS, SB_HEADS)
        copy_k = pltpu.make_async_copy(k_hbm.at[b, pl.ds(row0, blk * SB_HEADS)], kbuf, sem.at[0])
        copy_v = pltpu.make_async_copy(v_hbm.at[b, pl.ds(row0, blk * SB_HEADS)], vbuf, sem.at[1])
        copy_k.start()
        copy_v.start()
        copy_k.wait()
        copy_v.wait()
        unread = cols < end - start
        _sb_update(qs, [head_rows(kbuf, hd, 0, blk) for hd in range(SB_HEADS)],
                   [head_rows(vbuf, hd, 0, blk) for hd in range(SB_HEADS)],
                   carry_scr, acc_scr, sfx, (unread,), False, cmin_scr)
        return j + 1, cmin_scr[0]

    lax.while_loop(cond, body, (jnp.int32(0), cmin_scr[0]))
    for hd in range(SB_HEADS):
        o_ref[0, :, _head(hd)] = acc_scr[hd].astype(o_ref.dtype)


def _sb_sample(sq, sk, sv, k_past, v_past):
    bsz, t, _ = sq.shape
    p = k_past.shape[1]
    blk = SB_BLOCK
    assert p % blk == 0 and p >= blk and t < blk and t % 16 == 0
    k_past = k_past.reshape(bsz, p * SB_HEADS, SB_DH)
    v_past = v_past.reshape(bsz, p * SB_HEADS, SB_DH)
    sfx = _suffix_matrix(blk)
    new_spec = pl.BlockSpec((1, t, SB_W), lambda b: (b, 0, 0))
    window_spec = pl.BlockSpec((1, blk * SB_HEADS, SB_DH), lambda b: (b, p // blk - 1, 0))
    hbm_spec = pl.BlockSpec(memory_space=pl.ANY)
    return pl.pallas_call(
        functools.partial(_sb_sample_kernel, n_past=p),
        grid=(bsz,),
        in_specs=[new_spec, new_spec, new_spec, window_spec, window_spec, hbm_spec, hbm_spec,
                  _const_spec(sfx.shape)],
        out_specs=new_spec,
        out_shape=jax.ShapeDtypeStruct((bsz, t, SB_W), BF16),
        scratch_shapes=[pltpu.VMEM((SB_HEADS, t, LANES), F32), pltpu.VMEM((SB_HEADS, t, SB_DH), F32),
                        pltpu.VMEM((blk * SB_HEADS, SB_DH), F32), pltpu.VMEM((blk * SB_HEADS, SB_DH), F32),
                        pltpu.SemaphoreType.DMA((2,)), pltpu.SMEM((1,), F32)],
        compiler_params=_params("arbitrary"),
        name="sb_sample",
    )(sq, sk, sv, k_past, v_past, k_past, v_past, sfx)


def _merge_ffn_kernel(h_ref, oa_ref, ob_ref, ga_ref, gb_ref, wa_ref, wb_ref, wm_ref, g2_ref, b2_ref,
                      wg_ref, wu_ref, wo_ref, g3_ref, b3_ref, o_ref):
    branch_a = _dot(oa_ref[...], wa_ref[...])
    branch_b = _dot(ob_ref[...], wb_ref[...])
    merged = jax.nn.sigmoid(ga_ref[...]) * branch_a + jax.nn.sigmoid(gb_ref[...]) * branch_b
    mix = _dot(merged.astype(BF16), wm_ref[...])
    h2 = _layer_norm(DN_ALPHA * h_ref[...] + mix, g2_ref[...], b2_ref[...])
    o_ref[...] = _swiglu_ln(h2, wg_ref, wu_ref, wo_ref, g3_ref, b3_ref)


def _merge_ffn(groups, consts):
    outs = _rowwise_call(_merge_ffn_kernel, groups, consts, [(D_MODEL, F32)], TAIL_ROW_TILE, "merge_ffn")
    return [o[0] for o in outs]


def _cast_kernel(*refs, plans):
    n_in = len(plans)
    outs = iter(refs[n_in:])
    for ref, plan in zip(refs[:n_in], plans):
        x = ref[...]
        for lo, hi, keep in plan:
            y = x[:, lo:hi]
            if keep < hi - lo:
                y = jnp.where(lax.broadcasted_iota(jnp.int32, y.shape, 1) < keep, y, 0.0)
            next(outs)[...] = y.astype(BF16)


def _cast_weights(arrays, plans, rows_per_step):
    rows = arrays[0].shape[0]
    assert all(a.shape[0] == rows for a in arrays) and rows % rows_per_step == 0
    out_shape = [jax.ShapeDtypeStruct((rows, hi - lo), BF16) for plan in plans for lo, hi, _ in plan]
    return pl.pallas_call(
        functools.partial(_cast_kernel, plans=plans),
        grid=(rows // rows_per_step,),
        in_specs=[pl.BlockSpec((rows_per_step, a.shape[1]), lambda i: (i, 0)) for a in arrays],
        out_specs=[pl.BlockSpec((rows_per_step, o.shape[1]), lambda i: (i, 0)) for o in out_shape],
        out_shape=out_shape,
        compiler_params=_params("parallel"),
        name="cast_weights",
    )(*arrays)


def _row(v):
    return v.reshape(1, -1)


def _layer(xs, s0s, pasts, wts):
    shapes = [x.shape[:2] for x in xs]
    hs = _ffn_ln([x.reshape(-1, D_MODEL) for x in xs], wts["ffn1"])
    projected = _proj(hs, wts["proj"])
    tails, states, new_kv = [], [], []
    for (bsz, t), h, s0, past, outs in zip(shapes, hs, s0s, pasts, projected):
        gq, gk, gv, gr, la, sq, sk, sv, ga, gb, skb, svb = outs

        def seq(a):
            return a.reshape(bsz, t, a.shape[-1])

        o_a, s_new = _gla(seq(gq), seq(gk), seq(gv), seq(la), seq(gr), wts["gla_norm"], s0)
        if past is None:
            o_b = _sb_prompt(seq(sq), seq(skb), seq(svb))
        else:
            o_b = _sb_sample(seq(sq), seq(skb), seq(svb), *past)
        tails.append([h, o_a.reshape(-1, GLA_V), o_b.reshape(-1, SB_W), ga, gb])
        states.append(s_new)
        new_kv.append((sk.reshape(bsz, t, SB_HEADS, SB_DH), sv.reshape(bsz, t, SB_HEADS, SB_DH)))
    ys = _merge_ffn(tails, wts["merge"] + wts["ffn2"])
    return [(y.reshape(bsz, t, D_MODEL), s_new, k_new, v_new)
            for (bsz, t), y, s_new, (k_new, v_new) in zip(shapes, ys, states, new_kv)]


def kernel(x_prompt, x_sample, state_gla, cache_sb_k, cache_sb_v, ffn1_w_in, ffn1_w_out, ln1_g, ln1_b, w_in, w_gla_gate_up, b_gla_gate, g_gla_norm, w_gla_o, w_sb_o, w_out, ln2_g, ln2_b, ffn2_w_in, ffn2_w_out, ln3_g, ln3_b):
    xp, xs = x_prompt, x_sample
    bp = xp.shape[0]
    outs = [[] for _ in range(6)]
    lr_lo = 2 * GLA_QK + 2 * GLA_V
    lr_hi = lr_lo + GLA_GATE_RANK
    whole = ((0, D_MODEL, D_MODEL),)
    halves = ((0, D_FF, D_FF), (D_FF, 2 * D_FF, D_FF))
    for l in range(DEPTH):
        wg1, wu1, wg2, wu2, w_ga, w_sb, w_o = _cast_weights(
            [ffn1_w_in[l], ffn2_w_in[l], w_gla_o[l], w_sb_o[l], w_out[l]],
            (halves, halves, whole, whole, whole), CAST_ROWS)
        w_in_t = jnp.swapaxes(w_in[l], 0, 1)
        w_a = w_in_t[:lr_lo].astype(BF16)
        w_lr = jnp.pad(w_in_t[lr_lo:lr_hi], ((0, LANES - GLA_GATE_RANK), (0, 0))).astype(BF16)
        w_b = w_in_t[lr_hi:].astype(BF16)
        wo1, wo2 = _cast_weights([ffn1_w_out[l], ffn2_w_out[l]], (whole, whole), D_FF // 4)
        w_up = jnp.pad(w_gla_gate_up[l], ((0, LANES - GLA_GATE_RANK), (0, 0))).astype(BF16)
        wts = {
            "ffn1": (wg1, wu1, wo1, _row(ln1_g[l]), _row(ln1_b[l])),
            "proj": (w_a, w_b, w_lr, w_up, _row(b_gla_gate[l])),
            "gla_norm": _row(g_gla_norm[l]),
            "merge": (w_ga, w_sb, w_o, _row(ln2_g[l]), _row(ln2_b[l])),
            "ffn2": (wg2, wu2, wo2, _row(ln3_g[l]), _row(ln3_b[l])),
        }
        s0_p = jnp.zeros((bp, GLA_HEADS, GLA_DK, GLA_DV), F32)
        (xp, sp, kp, vp), (xs, ss, ksn, vsn) = _layer(
            [xp, xs], [s0_p, state_gla[l]], [None, (cache_sb_k[l], cache_sb_v[l])], wts)
        for lst, val in zip(outs, (sp, kp, vp, ss, ksn, vsn)):
            lst.append(val)
    return (xp, xs, *(jnp.stack(o) for o in outs))
```

```python
import functools

import jax
import jax.numpy as jnp
from jax import lax
from jax.experimental import pallas as pl
from jax.experimental.pallas import tpu as pltpu

F32 = jnp.float32
BF16 = jnp.bfloat16

D_MODEL = 1024
CHUNK = 64
GLA_HEADS = 4
GLA_DK = 128
GLA_DV = 256
GLA_QK = GLA_HEADS * GLA_DK
GLA_V = GLA_HEADS * GLA_DV
GLA_GATE_RANK = 16
GLA_GATE_TEMP = 16.0
SB_HEADS = 8
SB_DH = 128
SB_W = SB_HEADS * SB_DH
D_FF = 2816
LN_EPS = 1e-5
DEPTH = 1
DN_ALPHA = (2 * DEPTH) ** 0.25

LOG2E = 1.4426950408889634
LANES = 128
GLA_ROWS = 512
GLA_BATCHED_ROWS = 256
ROW_TILE = 512
PROJ_ROW_TILE = 256
CAST_ROWS = 64
TAIL_ROW_TILE = 256
SB_BLOCK = 256
SB_QBLOCK = 256
SB_LOOKBACK = 1
SB_GROUP = 8
SB_DEAD = 110.0
VMEM_LIMIT = 56 * 1024 * 1024


def _layer_norm(r, g, b):
    mu = jnp.mean(r, axis=-1, keepdims=True)
    d = r - mu
    var = jnp.mean(d * d, axis=-1, keepdims=True)
    return d * lax.rsqrt(var + LN_EPS) * g + b


def _softplus_neg_abs(z):
    return jnp.log(1.0 + jnp.exp(-jnp.abs(z)))


def _split_hi_lo(x):
    hi = x.astype(BF16)
    return hi, (x - hi.astype(F32)).astype(BF16)


def _dot(a, b):
    return jnp.dot(a, b, preferred_element_type=F32)


def _dot_nt(a, b):
    return lax.dot_general(a, b, (((1,), (1,)), ((), ())), preferred_element_type=F32)


def _dot_tn(a, b):
    return lax.dot_general(a, b, (((0,), (0,)), ((), ())), preferred_element_type=F32)


def _const_spec(shape):
    nd = len(shape)
    return pl.BlockSpec(shape, lambda *_: (0,) * nd, pipeline_mode=pl.Buffered(1))


def _params(*sem):
    return pltpu.CompilerParams(dimension_semantics=sem, vmem_limit_bytes=VMEM_LIMIT)


def _rowwise_call(body, row_inputs, consts, out_cols, tm, name):
    n_in, n_out, n_groups = len(row_inputs[0]), len(out_cols), len(row_inputs)
    rows = [g[0].shape[0] for g in row_inputs]
    tiles = [min(tm, r) for r in rows]
    steps = [r // t for r, t in zip(rows, tiles)]
    assert all(r % t == 0 for r, t in zip(rows, tiles))
    starts = [sum(steps[:g]) for g in range(n_groups)]

    def spec(g, width):
        return pl.BlockSpec((tiles[g], width), lambda i: (jnp.clip(i - starts[g], 0, steps[g] - 1), 0))

    def kern(*refs):
        i = pl.program_id(0)
        c0 = n_groups * n_in
        o0 = c0 + len(consts)
        for g in range(n_groups):
            @pl.when((i >= starts[g]) & (i < starts[g] + steps[g]))
            def _(g=g):
                body(*refs[g * n_in:(g + 1) * n_in], *refs[c0:o0], *refs[o0 + g * n_out:o0 + (g + 1) * n_out])

    outs = pl.pallas_call(
        kern,
        grid=(sum(steps),),
        in_specs=([spec(g, a.shape[1]) for g in range(n_groups) for a in row_inputs[g]]
                  + [_const_spec(c.shape) for c in consts]),
        out_specs=[spec(g, w) for g in range(n_groups) for w, _ in out_cols],
        out_shape=[jax.ShapeDtypeStruct((rows[g], w), d) for g in range(n_groups) for w, d in out_cols],
        compiler_params=_params("arbitrary"),
        name=name,
    )(*[a for g in row_inputs for a in g], *consts)
    return [outs[g * n_out:(g + 1) * n_out] for g in range(n_groups)]


def _swiglu_ln(x, wg_ref, wu_ref, wo_ref, g_ref, b_ref):
    xb = x.astype(BF16)
    gate = _dot(xb, wg_ref[...])
    up = _dot(xb, wu_ref[...])
    a = (gate * jax.nn.sigmoid(gate) * up).astype(BF16)
    return _layer_norm(DN_ALPHA * x + 0.5 * _dot(a, wo_ref[...]), g_ref[...], b_ref[...])


def _ffn_ln_kernel(x_ref, wg_ref, wu_ref, wo_ref, g_ref, b_ref, o_ref):
    o_ref[...] = _swiglu_ln(x_ref[...], wg_ref, wu_ref, wo_ref, g_ref, b_ref)


def _ffn_ln(xs, consts):
    outs = _rowwise_call(_ffn_ln_kernel, [[x] for x in xs], consts, [(D_MODEL, F32)], ROW_TILE, "ffn_ln")
    return [o[0] for o in outs]


_PROJ_GROUPS_A = (("gq", GLA_QK), ("gk", GLA_QK), ("gv", GLA_V), ("gr", GLA_V))
_PROJ_GROUPS_B = (("sq", SB_W), ("sk", SB_W), ("sv", SB_W), ("ga", D_MODEL), ("gb", D_MODEL))


def _proj_kernel(h_ref, wa_ref, wb_ref, wlr_ref, wup_ref, bup_ref,
                 gq_ref, gk_ref, gv_ref, gr_ref, la_ref, sq_ref, sk_ref, sv_ref, ga_ref, gb_ref,
                 skb_ref, svb_ref):
    hb = h_ref[...].astype(BF16)
    outs = dict(gq=gq_ref, gk=gk_ref, gv=gv_ref, gr=gr_ref, sq=sq_ref, sk=sk_ref, sv=sv_ref,
                ga=ga_ref, gb=gb_ref)
    for w_ref, groups in ((wa_ref, _PROJ_GROUPS_A), (wb_ref, _PROJ_GROUPS_B)):
        lo = 0
        for name, width in groups:
            y = _dot_nt(hb, w_ref[lo:lo + width, :])
            lo += width
            if name == "gq":
                y = y * (GLA_DK ** -0.5)
            if name == "sq":
                y = y * (SB_DH ** -0.5)
            if name in ("sq", "gv"):
                outs[name][...] = y.astype(BF16)
            else:
                outs[name][...] = y
            if name == "sk":
                skb_ref[...] = y.astype(BF16)
            if name == "sv":
                svb_ref[...] = y.astype(BF16)
    lr = _dot_nt(hb, wlr_ref[...]).astype(BF16)
    pre = _dot(lr, wup_ref[...]) + bup_ref[...]
    la_ref[...] = (jnp.minimum(pre, 0.0) - _softplus_neg_abs(pre)) * (1.0 / GLA_GATE_TEMP)


def _proj(hs, consts):
    widths = (GLA_QK, GLA_QK, GLA_V, GLA_V, GLA_QK, SB_W, SB_W, SB_W, D_MODEL, D_MODEL, SB_W, SB_W)
    dtypes = (F32, F32, BF16, F32, F32, BF16, F32, F32, F32, F32, BF16, BF16)
    return _rowwise_call(_proj_kernel, [[h] for h in hs], consts, list(zip(widths, dtypes)), PROJ_ROW_TILE, "proj")


def _gla_cumsum_matrix(tb):
    r = lax.broadcasted_iota(jnp.int32, (2 * tb, tb), 0)
    c = lax.broadcasted_iota(jnp.int32, (2 * tb, tb), 1)
    same_chunk = (r % tb) // CHUNK == c // CHUNK
    return (same_chunk & ((c <= r) | (r >= tb))).astype(BF16)


def _gla_kernel(q_ref, k_ref, v_ref, la_ref, gr_ref, gn_ref, s0_ref, cm_ref, o_ref, sout_ref, s_scr):
    t = pl.program_id(1)

    @pl.when(t == 0)
    def _():
        s_scr[...] = s0_ref[...]

    n_rows = q_ref.shape[0]
    n_seq = s_scr.shape[0]
    chunks_per_seq = n_rows // CHUNK // n_seq
    sb = cm_ref.shape[1]
    cm = cm_ref[...]
    gn = gn_ref[...]
    rows = lax.broadcasted_iota(jnp.int32, (sb, sb), 0)
    cols = lax.broadcasted_iota(jnp.int32, (sb, sb), 1)
    shift = CHUNK.bit_length() - 1
    causal = (cols <= rows) & ((rows >> shift) == (cols >> shift))

    for r0 in range(0, n_rows, sb):
        rb = slice(r0, r0 + sb)
        for h in range(GLA_HEADS):
            ks = slice(h * GLA_DK, (h + 1) * GLA_DK)
            vs = slice(h * GLA_DV, (h + 1) * GLA_DV)
            cs = _dot(cm, jnp.concatenate(_split_hi_lo(la_ref[rb, ks]), axis=1))
            cs = cs[:, :GLA_DK] + cs[:, GLA_DK:]
            b, b_tot = cs[:sb], cs[sb:]
            q = q_ref[rb, ks]
            k = k_ref[rb, ks]
            v = v_ref[rb, vs]
            qg = (q * jnp.exp(b)).astype(BF16)
            kg = (k * jnp.exp(-b)).astype(BF16)
            kd = (k * jnp.exp(b_tot - b)).astype(BF16)
            att = jnp.where(causal, _dot_nt(qg, kg), 0.0).astype(BF16)
            o_intra = _dot(att, v)
            for c in range(sb // CHUNK):
                seq = (r0 // CHUNK + c) // chunks_per_seq
                first_of_run = c == 0 or (r0 // CHUNK + c) % chunks_per_seq == 0
                last_of_run = c == sb // CHUNK - 1 or (r0 // CHUNK + c + 1) % chunks_per_seq == 0
                if first_of_run:
                    s = s_scr[seq, h]
                rs = slice(c * CHUNK, (c + 1) * CHUNK)
                ro = slice(r0 + c * CHUNK, r0 + (c + 1) * CHUNK)
                o = o_intra[rs] + _dot(qg[rs], s.astype(BF16))
                o = o * lax.rsqrt(jnp.mean(o * o, axis=-1, keepdims=True) + LN_EPS) * gn[:, vs]
                gate = gr_ref[ro, vs]
                o_ref[ro, vs] = (o * (gate * jax.nn.sigmoid(gate))).astype(BF16)
                decay = jnp.exp(jnp.broadcast_to(b_tot[c * CHUNK:c * CHUNK + 1], (GLA_DK, GLA_DK))).T
                s = jnp.concatenate([decay, decay], axis=1) * s + _dot_tn(kd[rs], v[rs])
                if last_of_run:
                    s_scr[seq, h] = s

    @pl.when(t == pl.num_programs(1) - 1)
    def _():
        sout_ref[...] = s_scr[...]


def _gla(q, k, v, la, gr, gn, s0):
    bsz, t, _ = q.shape
    n_rows = min(GLA_ROWS, bsz * t)
    n_seq = max(1, n_rows // t)
    n_inner = max(1, t // n_rows)
    assert (bsz * t) % n_rows == 0 and bsz % n_seq == 0 and t % CHUNK == 0 and (t % n_rows == 0 or n_rows % t == 0)
    cm = _gla_cumsum_matrix(min(n_rows, GLA_BATCHED_ROWS))

    def rows(width):
        return pl.BlockSpec((n_rows, width), lambda b, i: (b * n_inner + i, 0))

    def flat(a):
        return a.reshape(bsz * t, a.shape[-1])

    s_spec = pl.BlockSpec((n_seq, GLA_HEADS, GLA_DK, GLA_DV), lambda b, i: (b, 0, 0, 0))
    o, s_new = pl.pallas_call(
        _gla_kernel,
        grid=(bsz // n_seq, n_inner),
        in_specs=[rows(GLA_QK), rows(GLA_QK), rows(GLA_V), rows(GLA_QK), rows(GLA_V), _const_spec(gn.shape),
                  s_spec, _const_spec(cm.shape)],
        out_specs=[rows(GLA_V), s_spec],
        out_shape=[jax.ShapeDtypeStruct((bsz * t, GLA_V), BF16),
                   jax.ShapeDtypeStruct((bsz, GLA_HEADS, GLA_DK, GLA_DV), F32)],
        scratch_shapes=[pltpu.VMEM((n_seq, GLA_HEADS, GLA_DK, GLA_DV), F32)],
        compiler_params=_params("parallel", "arbitrary"),
        name="gla",
    )(flat(q), flat(k), flat(v), flat(la), flat(gr), gn, s0, cm)
    return o, s_new


def _suffix_matrix(tk):
    r = lax.broadcasted_iota(jnp.int32, (2 * tk, tk), 0) % tk
    c = lax.broadcasted_iota(jnp.int32, (2 * tk, tk), 1)
    return (r >= c).astype(BF16)


def _sb_update(qs, ks, vs, carry_ref, acc_ref, suffix, visible, first, cmin_ref=None):
    heads = range(len(qs))
    sub = suffix.shape[1]
    n_sub = ks[0].shape[0] // sub
    z = [_dot_nt(qs[hd], ks[hd]) for hd in heads]
    carry = [None if first else carry_ref[hd] for hd in heads]
    ws = [[None] * n_sub for _ in heads]
    for s in reversed(range(n_sub)):
        vis = None if visible is None else visible[s]
        zs = [z[hd][:, s * sub:(s + 1) * sub] for hd in heads]
        if vis is not None:
            zs = [jnp.where(vis, x, -jnp.inf) for x in zs]
        cs = [jnp.maximum(x, 0.0) + jnp.log(1.0 + jnp.exp2(jnp.abs(x) * -LOG2E)) for x in zs]
        sums = [_dot(jnp.concatenate(_split_hi_lo(c), axis=1), suffix) for c in cs]
        for hd in heads:
            e = zs[hd] - sums[hd]
            if carry[hd] is not None:
                e = e - jnp.concatenate([carry[hd]] * (sub // LANES), axis=1)
            ws[hd][s] = jnp.exp(e).astype(BF16)
            total = jnp.broadcast_to(jnp.sum(cs[hd], axis=-1, keepdims=True), (cs[hd].shape[0], LANES))
            carry[hd] = total if carry[hd] is None else carry[hd] + total
    if cmin_ref is not None:
        cmin_ref[0] = jnp.min(functools.reduce(jnp.minimum, carry))
    for hd in heads:
        pv = _dot(ws[hd][0] if n_sub == 1 else jnp.concatenate(ws[hd], axis=1), vs[hd])
        acc_ref[hd] = pv if first else acc_ref[hd] + pv
        carry_ref[hd] = carry[hd]


def _head(hd):
    return slice(hd * SB_DH, (hd + 1) * SB_DH)


def _sb_prompt_kernel(q_ref, k_ref, v_ref, sfx_ref, o_ref, carry_scr, acc_scr, cmin_scr):
    i = pl.program_id(2)
    tq = q_ref.shape[1]
    heads = q_ref.shape[2] // SB_DH
    sfx = sfx_ref[...]
    rows = lax.broadcasted_iota(jnp.int32, (tq, tq), 0)
    cols = lax.broadcasted_iota(jnp.int32, (tq, tq), 1)
    strictly_earlier = cols < rows
    qs = [q_ref[0, :, _head(hd)] for hd in range(heads)]

    def keys(ref, start, n):
        return [ref[0, pl.ds(start, n), _head(hd)] for hd in range(heads)]

    for n_prev in range(SB_LOOKBACK + 1):
        @pl.when((i == n_prev) if n_prev < SB_LOOKBACK else (i >= n_prev))
        def _(n_prev=n_prev):
            p0 = pl.multiple_of((i - n_prev) * tq, tq)
            n = (n_prev + 1) * tq
            _sb_update(qs, keys(k_ref, p0, n), keys(v_ref, p0, n), carry_scr, acc_scr, sfx,
                       (None,) * n_prev + (strictly_earlier,), True, cmin_scr)

    n_older = jnp.maximum(i - SB_LOOKBACK, 0)

    def cond(state):
        j, cmin = state
        return (j < n_older) & (cmin <= SB_DEAD)

    def body(state):
        j, _ = state
        k0 = pl.multiple_of((i - SB_LOOKBACK - 1 - j) * tq, tq)
        _sb_update(qs, keys(k_ref, k0, tq), keys(v_ref, k0, tq), carry_scr, acc_scr, sfx, None, False, cmin_scr)
        return j + 1, cmin_scr[0]

    lax.while_loop(cond, body, (jnp.int32(0), cmin_scr[0]))
    for hd in range(heads):
        o_ref[0, :, _head(hd)] = acc_scr[hd].astype(o_ref.dtype)


def _sb_prompt(sq, sk, sv):
    bsz, t, _ = sq.shape
    blk = SB_QBLOCK
    assert t % blk == 0
    width = SB_GROUP * SB_DH
    sfx = _suffix_matrix(blk)
    q_spec = pl.BlockSpec((1, blk, width), lambda b, g, i: (b, i, g))
    kv_spec = pl.BlockSpec((1, t, width), lambda b, g, i: (b, 0, g), pipeline_mode=pl.Buffered(1))
    return pl.pallas_call(
        _sb_prompt_kernel,
        grid=(bsz, SB_HEADS // SB_GROUP, t // blk),
        in_specs=[q_spec, kv_spec, kv_spec, _const_spec(sfx.shape)],
        out_specs=q_spec,
        out_shape=jax.ShapeDtypeStruct((bsz, t, SB_W), BF16),
        scratch_shapes=[pltpu.VMEM((SB_GROUP, blk, LANES), F32), pltpu.VMEM((SB_GROUP, blk, SB_DH), F32),
                        pltpu.SMEM((1,), F32)],
        compiler_params=_params("parallel", "parallel", "arbitrary"),
        name="sb_prompt",
    )(sq, sk, sv, sfx)


def _sb_sample_kernel(q_ref, kn_ref, vn_ref, kw_ref, vw_ref, k_hbm, v_hbm, sfx_ref, o_ref,
                      carry_scr, acc_scr, kbuf, vbuf, sem, cmin_scr, *, n_past):
    b = pl.program_id(0)
    tq = q_ref.shape[1]
    blk = SB_BLOCK
    win = blk - tq
    sfx = sfx_ref[...]
    rows = lax.broadcasted_iota(jnp.int32, (tq, blk), 0)
    cols = lax.broadcasted_iota(jnp.int32, (tq, blk), 1)

    def head_rows(ref, hd, first, n):
        return ref[pl.ds(first * SB_HEADS + hd, n, stride=SB_HEADS), :].astype(BF16)

    visible = (cols < win) | (cols - win < rows)
    qs = [q_ref[0, :, _head(hd)] for hd in range(SB_HEADS)]
    _sb_update(qs,
               [jnp.concatenate([head_rows(kw_ref.at[0], hd, blk - win, win), kn_ref[0, :, _head(hd)]], axis=0)
                for hd in range(SB_HEADS)],
               [jnp.concatenate([head_rows(vw_ref.at[0], hd, blk - win, win), vn_ref[0, :, _head(hd)]], axis=0)
                for hd in range(SB_HEADS)],
               carry_scr, acc_scr, sfx, (visible,), True, cmin_scr)

    n_left = n_past - win
    n_older = -(-n_left // blk)

    def cond(state):
        j, cmin = state
        return (j < n_older) & (cmin <= SB_DEAD)

    def body(state):
        j, _ = state
        end = n_left - j * blk
        start = jnp.maximum(end - blk, 0)
        row0 = pl.multiple_of(start * SB_HEADS, SB_HEADS)
        copy_k = pltpu.make_async_copy(k_hbm.at[b, pl.ds(row0, blk * SB_HEADS)], kbuf, sem.at[0])
        copy_v = pltpu.make_async_copy(v_hbm.at[b, pl.ds(row0, blk * SB_HEADS)], vbuf, sem.at[1])
        copy_k.start()
        copy_v.start()
        copy_k.wait()
        copy_v.wait()
        unread = cols < end - start
        _sb_update(qs, [head_rows(kbuf, hd, 0, blk) for hd in range(SB_HEADS)],
                   [head_rows(vbuf, hd, 0, blk) for hd in range(SB_HEADS)],
                   carry_scr, acc_scr, sfx, (unread,), False, cmin_scr)
        return j + 1, cmin_scr[0]

    lax.while_loop(cond, body, (jnp.int32(0), cmin_scr[0]))
    for hd in range(SB_HEADS):
        o_ref[0, :, _head(hd)] = acc_scr[hd].astype(o_ref.dtype)


def _sb_sample(sq, sk, sv, k_past, v_past):
    bsz, t, _ = sq.shape
    p = k_past.shape[1]
    blk = SB_BLOCK
    assert p % blk == 0 and p >= blk and t < blk and t % 16 == 0
    k_past = k_past.reshape(bsz, p * SB_HEADS, SB_DH)
    v_past = v_past.reshape(bsz, p * SB_HEADS, SB_DH)
    sfx = _suffix_matrix(blk)
    new_spec = pl.BlockSpec((1, t, SB_W), lambda b: (b, 0, 0))
    window_spec = pl.BlockSpec((1, blk * SB_HEADS, SB_DH), lambda b: (b, p // blk - 1, 0))
    hbm_spec = pl.BlockSpec(memory_space=pl.ANY)
    return pl.pallas_call(
        functools.partial(_sb_sample_kernel, n_past=p),
        grid=(bsz,),
        in_specs=[new_spec, new_spec, new_spec, window_spec, window_spec, hbm_spec, hbm_spec,
                  _const_spec(sfx.shape)],
        out_specs=new_spec,
        out_shape=jax.ShapeDtypeStruct((bsz, t, SB_W), BF16),
        scratch_shapes=[pltpu.VMEM((SB_HEADS, t, LANES), F32), pltpu.VMEM((SB_HEADS, t, SB_DH), F32),
                        pltpu.VMEM((blk * SB_HEADS, SB_DH), F32), pltpu.VMEM((blk * SB_HEADS, SB_DH), F32),
                        pltpu.SemaphoreType.DMA((2,)), pltpu.SMEM((1,), F32)],
        compiler_params=_params("arbitrary"),
        name="sb_sample",
    )(sq, sk, sv, k_past, v_past, k_past, v_past, sfx)


def _merge_ffn_kernel(h_ref, oa_ref, ob_ref, ga_ref, gb_ref, wa_ref, wb_ref, wm_ref, g2_ref, b2_ref,
                      wg_ref, wu_ref, wo_ref, g3_ref, b3_ref, o_ref):
    branch_a = _dot(oa_ref[...], wa_ref[...])
    branch_b = _dot(ob_ref[...], wb_ref[...])
    merged = jax.nn.sigmoid(ga_ref[...]) * branch_a + jax.nn.sigmoid(gb_ref[...]) * branch_b
    mix = _dot(merged.astype(BF16), wm_ref[...])
    h2 = _layer_norm(DN_ALPHA * h_ref[...] + mix, g2_ref[...], b2_ref[...])
    o_ref[...] = _swiglu_ln(h2, wg_ref, wu_ref, wo_ref, g3_ref, b3_ref)


def _merge_ffn(groups, consts):
    outs = _rowwise_call(_merge_ffn_kernel, groups, consts, [(D_MODEL, F32)], TAIL_ROW_TILE, "merge_ffn")
    return [o[0] for o in outs]


def _cast_kernel(*refs, plans):
    n_in = len(plans)
    outs = iter(refs[n_in:])
    for ref, plan in zip(refs[:n_in], plans):
        x = ref[...]
        for lo, hi, keep in plan:
            y = x[:, lo:hi]
            if keep < hi - lo:
                y = jnp.where(lax.broadcasted_iota(jnp.int32, y.shape, 1) < keep, y, 0.0)
            next(outs)[...] = y.astype(BF16)


def _cast_weights(arrays, plans, rows_per_step):
    rows = arrays[0].shape[0]
    assert all(a.shape[0] == rows for a in arrays) and rows % rows_per_step == 0
    out_shape = [jax.ShapeDtypeStruct((rows, hi - lo), BF16) for plan in plans for lo, hi, _ in plan]
    return pl.pallas_call(
        functools.partial(_cast_kernel, plans=plans),
        grid=(rows // rows_per_step,),
        in_specs=[pl.BlockSpec((rows_per_step, a.shape[1]), lambda i: (i, 0)) for a in arrays],
        out_specs=[pl.BlockSpec((rows_per_step, o.shape[1]), lambda i: (i, 0)) for o in out_shape],
        out_shape=out_shape,
        compiler_params=_params("parallel"),
        name="cast_weights",
    )(*arrays)


def _row(v):
    return v.reshape(1, -1)


def _layer(xs, s0s, pasts, wts):
    shapes = [x.shape[:2] for x in xs]
    hs = _ffn_ln([x.reshape(-1, D_MODEL) for x in xs], wts["ffn1"])
    projected = _proj(hs, wts["proj"])
    tails, states, new_kv = [], [], []
    for (bsz, t), h, s0, past, outs in zip(shapes, hs, s0s, pasts, projected):
        gq, gk, gv, gr, la, sq, sk, sv, ga, gb, skb, svb = outs

        def seq(a):
            return a.reshape(bsz, t, a.shape[-1])

        o_a, s_new = _gla(seq(gq), seq(gk), seq(gv), seq(la), seq(gr), wts["gla_norm"], s0)
        if past is None:
            o_b = _sb_prompt(seq(sq), seq(skb), seq(svb))
        else:
            o_b = _sb_sample(seq(sq), seq(skb), seq(svb), *past)
        tails.append([h, o_a.reshape(-1, GLA_V), o_b.reshape(-1, SB_W), ga, gb])
        states.append(s_new)
        new_kv.append((sk.reshape(bsz, t, SB_HEADS, SB_DH), sv.reshape(bsz, t, SB_HEADS, SB_DH)))
    ys = _merge_ffn(tails, wts["merge"] + wts["ffn2"])
    return [(y.reshape(bsz, t, D_MODEL), s_new, k_new, v_new)
            for (bsz, t), y, s_new, (k_new, v_new) in zip(shapes, ys, states, new_kv)]


def kernel(x_prompt, x_sample, state_gla, cache_sb_k, cache_sb_v, ffn1_w_in, ffn1_w_out, ln1_g, ln1_b, w_in, w_gla_gate_up, b_gla_gate, g_gla_norm, w_gla_o, w_sb_o, w_out, ln2_g, ln2_b, ffn2_w_in, ffn2_w_out, ln3_g, ln3_b):
    xp, xs = x_prompt, x_sample
    bp = xp.shape[0]
    outs = [[] for _ in range(6)]
    lr_lo = 2 * GLA_QK + 2 * GLA_V
    lr_hi = lr_lo + GLA_GATE_RANK
    whole = ((0, D_MODEL, D_MODEL),)
    halves = ((0, D_FF, D_FF), (D_FF, 2 * D_FF, D_FF))
    for l in range(DEPTH):
        wg1, wu1, wg2, wu2, w_ga, w_sb, w_o = _cast_weights(
            [ffn1_w_in[l], ffn2_w_in[l], w_gla_o[l], w_sb_o[l], w_out[l]],
            (halves, halves, whole, whole, whole), CAST_ROWS)
        w_in_t = jnp.swapaxes(w_in[l], 0, 1)
        w_a = w_in_t[:lr_lo].astype(BF16)
        w_lr = jnp.pad(w_in_t[lr_lo:lr_hi], ((0, LANES - GLA_GATE_RANK), (0, 0))).astype(BF16)
        w_b = w_in_t[lr_hi:].astype(BF16)
        wo1, wo2 = _cast_weights([ffn1_w_out[l], ffn2_w_out[l]], (whole, whole), D_FF // 4)
        w_up = jnp.pad(w_gla_gate_up[l], ((0, LANES - GLA_GATE_RANK), (0, 0))).astype(BF16)
        wts = {
            "ffn1": (wg1, wu1, wo1, _row(ln1_g[l]), _row(ln1_b[l])),
            "proj": (w_a, w_b, w_lr, w_up, _row(b_gla_gate[l])),
            "gla_norm": _row(g_gla_norm[l]),
            "merge": (w_ga, w_sb, w_o, _row(ln2_g[l]), _row(ln2_b[l])),
            "ffn2": (wg2, wu2, wo2, _row(ln3_g[l]), _row(ln3_b[l])),
        }
        s0_p = jnp.zeros((bp, GLA_HEADS, GLA_DK, GLA_DV), F32)
        (xp, sp, kp, vp), (xs, ss, ksn, vsn) = _layer(
            [xp, xs], [s0_p, state_gla[l]], [None, (cache_sb_k[l], cache_sb_v[l])], wts)
        for lst, val in zip(outs, (sp, kp, vp, ss, ksn, vsn)):
            lst.append(val)
    return (xp, xs, *(jnp.stack(o) for o in outs))
```

```python
import functools

import jax
import jax.numpy as jnp
from jax import lax
from jax.experimental import pallas as pl
from jax.experimental.pallas import tpu as pltpu

F32 = jnp.float32
BF16 = jnp.bfloat16

D_MODEL = 1024
CHUNK = 64
GLA_HEADS = 4
GLA_DK = 128
GLA_DV = 256
GLA_QK = GLA_HEADS * GLA_DK
GLA_V = GLA_HEADS * GLA_DV
GLA_GATE_RANK = 16
GLA_GATE_TEMP = 16.0
SB_HEADS = 8
SB_DH = 128
SB_W = SB_HEADS * SB_DH
D_FF = 2816
LN_EPS = 1e-5
DEPTH = 1
DN_ALPHA = (2 * DEPTH) ** 0.25

LOG2E = 1.4426950408889634
LANES = 128
GLA_ROWS = 512
GLA_BATCHED_ROWS = 256
ROW_TILE = 512
PROJ_ROW_TILE = 256
CAST_ROWS = 64
TAIL_ROW_TILE = 256
SB_BLOCK = 256
SB_SAMPLE_SEQS = 2
SB_QBLOCK = 256
SB_GROUP = 8
SB_DEAD = 110.0
VMEM_LIMIT = 56 * 1024 * 1024


def _layer_norm(r, g, b):
    mu = jnp.mean(r, axis=-1, keepdims=True)
    d = r - mu
    var = jnp.mean(d * d, axis=-1, keepdims=True)
    return d * lax.rsqrt(var + LN_EPS) * g + b


def _softplus_neg_abs(z):
    return jnp.log(1.0 + jnp.exp(-jnp.abs(z)))


def _split_hi_lo(x):
    hi = x.astype(BF16)
    return hi, (x - hi.astype(F32)).astype(BF16)


def _dot(a, b):
    return jnp.dot(a, b, preferred_element_type=F32)


def _dot_nt(a, b):
    return lax.dot_general(a, b, (((1,), (1,)), ((), ())), preferred_element_type=F32)


def _dot_tn(a, b):
    return lax.dot_general(a, b, (((0,), (0,)), ((), ())), preferred_element_type=F32)


def _const_spec(shape):
    nd = len(shape)
    return pl.BlockSpec(shape, lambda *_: (0,) * nd, pipeline_mode=pl.Buffered(1))


def _params(*sem):
    return pltpu.CompilerParams(dimension_semantics=sem, vmem_limit_bytes=VMEM_LIMIT)


def _rowwise_call(body, row_inputs, consts, out_cols, tm, name):
    n_in, n_out, n_groups = len(row_inputs[0]), len(out_cols), len(row_inputs)
    rows = [g[0].shape[0] for g in row_inputs]
    tiles = [min(tm, r) for r in rows]
    steps = [r // t for r, t in zip(rows, tiles)]
    assert all(r % t == 0 for r, t in zip(rows, tiles))
    starts = [sum(steps[:g]) for g in range(n_groups)]

    def spec(g, width):
        return pl.BlockSpec((tiles[g], width), lambda i: (jnp.clip(i - starts[g], 0, steps[g] - 1), 0))

    def kern(*refs):
        i = pl.program_id(0)
        c0 = n_groups * n_in
        o0 = c0 + len(consts)
        for g in range(n_groups):
            @pl.when((i >= starts[g]) & (i < starts[g] + steps[g]))
            def _(g=g):
                body(*refs[g * n_in:(g + 1) * n_in], *refs[c0:o0], *refs[o0 + g * n_out:o0 + (g + 1) * n_out])

    outs = pl.pallas_call(
        kern,
        grid=(sum(steps),),
        in_specs=([spec(g, a.shape[1]) for g in range(n_groups) for a in row_inputs[g]]
                  + [_const_spec(c.shape) for c in consts]),
        out_specs=[spec(g, w) for g in range(n_groups) for w, _ in out_cols],
        out_shape=[jax.ShapeDtypeStruct((rows[g], w), d) for g in range(n_groups) for w, d in out_cols],
        compiler_params=_params("arbitrary"),
        name=name,
    )(*[a for g in row_inputs for a in g], *consts)
    return [outs[g * n_out:(g + 1) * n_out] for g in range(n_groups)]


def _swiglu_ln(x, wg_ref, wu_ref, wo_ref, g_ref, b_ref):
    xb = x.astype(BF16)
    gate = _dot(xb, wg_ref[...])
    up = _dot(xb, wu_ref[...])
    a = (gate * jax.nn.sigmoid(gate) * up).astype(BF16)
    return _layer_norm(DN_ALPHA * x + 0.5 * _dot(a, wo_ref[...]), g_ref[...], b_ref[...])


def _ffn_ln_kernel(x_ref, wg_ref, wu_ref, wo_ref, g_ref, b_ref, o_ref):
    o_ref[...] = _swiglu_ln(x_ref[...], wg_ref, wu_ref, wo_ref, g_ref, b_ref)


def _ffn_ln(xs, consts):
    outs = _rowwise_call(_ffn_ln_kernel, [[x] for x in xs], consts, [(D_MODEL, F32)], ROW_TILE, "ffn_ln")
    return [o[0] for o in outs]


_PROJ_GROUPS_A = (("gq", GLA_QK), ("gk", GLA_QK), ("gv", GLA_V), ("gr", GLA_V))
_PROJ_GROUPS_B = (("sq", SB_W), ("sk", SB_W), ("sv", SB_W), ("ga", D_MODEL), ("gb", D_MODEL))


def _proj_kernel(h_ref, wa_ref, wb_ref, wlr_ref, wup_ref, bup_ref,
                 gq_ref, gk_ref, gv_ref, gr_ref, la_ref, sq_ref, sk_ref, sv_ref, ga_ref, gb_ref,
                 skb_ref, svb_ref):
    hb = h_ref[...].astype(BF16)
    outs = dict(gq=gq_ref, gk=gk_ref, gv=gv_ref, gr=gr_ref, sq=sq_ref, sk=sk_ref, sv=sv_ref,
                ga=ga_ref, gb=gb_ref)
    for w_ref, groups in ((wa_ref, _PROJ_GROUPS_A), (wb_ref, _PROJ_GROUPS_B)):
        lo = 0
        for name, width in groups:
            y = _dot_nt(hb, w_ref[lo:lo + width, :])
            lo += width
            if name == "gq":
                y = y * (GLA_DK ** -0.5)
            if name == "sq":
                y = y * (SB_DH ** -0.5)
            if name in ("sq", "gv"):
                outs[name][...] = y.astype(BF16)
            else:
                outs[name][...] = y
            if name == "sk":
                skb_ref[...] = y.astype(BF16)
            if name == "sv":
                svb_ref[...] = y.astype(BF16)
    lr = _dot_nt(hb, wlr_ref[...]).astype(BF16)
    pre = _dot(lr, wup_ref[...]) + bup_ref[...]
    la_ref[...] = (jnp.minimum(pre, 0.0) - _softplus_neg_abs(pre)) * (1.0 / GLA_GATE_TEMP)


def _proj(hs, consts):
    widths = (GLA_QK, GLA_QK, GLA_V, GLA_V, GLA_QK, SB_W, SB_W, SB_W, D_MODEL, D_MODEL, SB_W, SB_W)
    dtypes = (F32, F32, BF16, F32, F32, BF16, F32, F32, F32, F32, BF16, BF16)
    return _rowwise_call(_proj_kernel, [[h] for h in hs], consts, list(zip(widths, dtypes)), PROJ_ROW_TILE, "proj")


def _gla_cumsum_matrix(tb):
    r = lax.broadcasted_iota(jnp.int32, (2 * tb, tb), 0)
    c = lax.broadcasted_iota(jnp.int32, (2 * tb, tb), 1)
    same_chunk = (r % tb) // CHUNK == c // CHUNK
    return (same_chunk & ((c <= r) | (r >= tb))).astype(BF16)


def _gla_kernel(q_ref, k_ref, v_ref, la_ref, gr_ref, gn_ref, s0_ref, cm_ref, o_ref, sout_ref, s_scr):
    t = pl.program_id(1)

    @pl.when(t == 0)
    def _():
        s_scr[...] = s0_ref[...]

    n_rows = q_ref.shape[0]
    n_seq = s_scr.shape[0]
    chunks_per_seq = n_rows // CHUNK // n_seq
    sb = cm_ref.shape[1]
    cm = cm_ref[...]
    gn = gn_ref[...]
    rows = lax.broadcasted_iota(jnp.int32, (sb, sb), 0)
    cols = lax.broadcasted_iota(jnp.int32, (sb, sb), 1)
    shift = CHUNK.bit_length() - 1
    causal = (cols <= rows) & ((rows >> shift) == (cols >> shift))

    for r0 in range(0, n_rows, sb):
        rb = slice(r0, r0 + sb)
        for h in range(GLA_HEADS):
            ks = slice(h * GLA_DK, (h + 1) * GLA_DK)
            vs = slice(h * GLA_DV, (h + 1) * GLA_DV)
            cs = _dot(cm, jnp.concatenate(_split_hi_lo(la_ref[rb, ks]), axis=1))
            cs = cs[:, :GLA_DK] + cs[:, GLA_DK:]
            b, b_tot = cs[:sb], cs[sb:]
            q = q_ref[rb, ks]
            k = k_ref[rb, ks]
            v = v_ref[rb, vs]
            qg = (q * jnp.exp(b)).astype(BF16)
            kg = (k * jnp.exp(-b)).astype(BF16)
            kd = (k * jnp.exp(b_tot - b)).astype(BF16)
            att = jnp.where(causal, _dot_nt(qg, kg), 0.0).astype(BF16)
            o_intra = _dot(att, v)
            for c in range(sb // CHUNK):
                seq = (r0 // CHUNK + c) // chunks_per_seq
                first_of_run = c == 0 or (r0 // CHUNK + c) % chunks_per_seq == 0
                last_of_run = c == sb // CHUNK - 1 or (r0 // CHUNK + c + 1) % chunks_per_seq == 0
                if first_of_run:
                    s = s_scr[seq, h]
                rs = slice(c * CHUNK, (c + 1) * CHUNK)
                ro = slice(r0 + c * CHUNK, r0 + (c + 1) * CHUNK)
                o = o_intra[rs] + _dot(qg[rs], s.astype(BF16))
                o = o * lax.rsqrt(jnp.mean(o * o, axis=-1, keepdims=True) + LN_EPS) * gn[:, vs]
                gate = gr_ref[ro, vs]
                o_ref[ro, vs] = (o * (gate * jax.nn.sigmoid(gate))).astype(BF16)
                decay = jnp.exp(jnp.broadcast_to(b_tot[c * CHUNK:c * CHUNK + 1], (GLA_DK, GLA_DK))).T
                s = jnp.concatenate([decay, decay], axis=1) * s + _dot_tn(kd[rs], v[rs])
                if last_of_run:
                    s_scr[seq, h] = s

    @pl.when(t == pl.num_programs(1) - 1)
    def _():
        sout_ref[...] = s_scr[...]


def _gla(q, k, v, la, gr, gn, s0):
    bsz, t, _ = q.shape
    n_rows = min(GLA_ROWS, bsz * t)
    n_seq = max(1, n_rows // t)
    n_inner = max(1, t // n_rows)
    assert (bsz * t) % n_rows == 0 and bsz % n_seq == 0 and t % CHUNK == 0 and (t % n_rows == 0 or n_rows % t == 0)
    cm = _gla_cumsum_matrix(min(n_rows, GLA_BATCHED_ROWS))

    def rows(width):
        return pl.BlockSpec((n_rows, width), lambda b, i: (b * n_inner + i, 0))

    def flat(a):
        return a.reshape(bsz * t, a.shape[-1])

    s_spec = pl.BlockSpec((n_seq, GLA_HEADS, GLA_DK, GLA_DV), lambda b, i: (b, 0, 0, 0))
    o, s_new = pl.pallas_call(
        _gla_kernel,
        grid=(bsz // n_seq, n_inner),
        in_specs=[rows(GLA_QK), rows(GLA_QK), rows(GLA_V), rows(GLA_QK), rows(GLA_V), _const_spec(gn.shape),
                  s_spec, _const_spec(cm.shape)],
        out_specs=[rows(GLA_V), s_spec],
        out_shape=[jax.ShapeDtypeStruct((bsz * t, GLA_V), BF16),
                   jax.ShapeDtypeStruct((bsz, GLA_HEADS, GLA_DK, GLA_DV), F32)],
        scratch_shapes=[pltpu.VMEM((n_seq, GLA_HEADS, GLA_DK, GLA_DV), F32)],
        compiler_params=_params("parallel", "arbitrary"),
        name="gla",
    )(flat(q), flat(k), flat(v), flat(la), flat(gr), gn, s0, cm)
    return o, s_new


def _suffix_matrix(tk):
    r = lax.broadcasted_iota(jnp.int32, (2 * tk, tk), 0) % tk
    c = lax.broadcasted_iota(jnp.int32, (2 * tk, tk), 1)
    return (r >= c).astype(BF16)


def _sb_update(qs, ks, vs, carry_ref, acc_ref, suffix, visible, first, cmin_ref=None):
    heads = range(len(qs))
    sub = suffix.shape[1]
    n_sub = ks[0].shape[0] // sub
    z = [_dot_nt(qs[hd], ks[hd]) for hd in heads]
    carry = [None if first else carry_ref[hd] for hd in heads]
    ws = [[None] * n_sub for _ in heads]
    for s in reversed(range(n_sub)):
        vis = None if visible is None else visible[s]
        zs = [z[hd][:, s * sub:(s + 1) * sub] for hd in heads]
        if vis is not None:
            zs = [jnp.where(vis, x, -jnp.inf) for x in zs]
        cs = [jnp.maximum(x, 0.0) + jnp.log(1.0 + jnp.exp2(jnp.abs(x) * -LOG2E)) for x in zs]
        sums = [_dot(jnp.concatenate(_split_hi_lo(c), axis=1), suffix) for c in cs]
        for hd in heads:
            e = zs[hd] - sums[hd]
            if carry[hd] is not None:
                e = e - jnp.concatenate([carry[hd]] * (sub // LANES), axis=1)
            ws[hd][s] = jnp.exp(e).astype(BF16)
            total = jnp.broadcast_to(jnp.sum(cs[hd], axis=-1, keepdims=True), (cs[hd].shape[0], LANES))
            carry[hd] = total if carry[hd] is None else carry[hd] + total
    if cmin_ref is not None:
        cmin_ref[0] = jnp.min(functools.reduce(jnp.minimum, carry))
    for hd in heads:
        pv = _dot(ws[hd][0] if n_sub == 1 else jnp.concatenate(ws[hd], axis=1), vs[hd])
        acc_ref[hd] = pv if first else acc_ref[hd] + pv
        carry_ref[hd] = carry[hd]


def _head(hd):
    return slice(hd * SB_DH, (hd + 1) * SB_DH)


def _sb_prompt_kernel(q_ref, kd_ref, vd_ref, kp_ref, vp_ref, k_hbm, v_hbm, sfx_ref, o_ref,
                      carry_scr, acc_scr, kbuf, vbuf, sem, cmin_scr):
    b, g, i = pl.program_id(0), pl.program_id(1), pl.program_id(2)
    tq = q_ref.shape[1]
    width = q_ref.shape[2]
    heads = width // SB_DH
    sfx = sfx_ref[...]
    rows = lax.broadcasted_iota(jnp.int32, (tq, tq), 0)
    cols = lax.broadcasted_iota(jnp.int32, (tq, tq), 1)
    strictly_earlier = cols < rows
    qs = [q_ref[0, :, _head(hd)] for hd in range(heads)]

    @pl.when(i == 0)
    def _():
        _sb_update(qs, [kd_ref[0, :, _head(hd)] for hd in range(heads)],
                   [vd_ref[0, :, _head(hd)] for hd in range(heads)],
                   carry_scr, acc_scr, sfx, (strictly_earlier,), True, cmin_scr)

    @pl.when(i > 0)
    def _():
        _sb_update(qs, [jnp.concatenate([kp_ref[0, :, _head(hd)], kd_ref[0, :, _head(hd)]], axis=0)
                        for hd in range(heads)],
                   [jnp.concatenate([vp_ref[0, :, _head(hd)], vd_ref[0, :, _head(hd)]], axis=0)
                    for hd in range(heads)],
                   carry_scr, acc_scr, sfx, (None, strictly_earlier), True, cmin_scr)

    n_older = jnp.maximum(i - 1, 0)

    def cond(state):
        j, cmin = state
        return (j < n_older) & (cmin <= SB_DEAD)

    def body(state):
        j, _ = state
        k0 = pl.multiple_of((i - 2 - j) * tq, tq)
        c0 = pl.multiple_of(g * width, LANES)
        copy_k = pltpu.make_async_copy(k_hbm.at[b, pl.ds(k0, tq), pl.ds(c0, width)], kbuf, sem.at[0])
        copy_v = pltpu.make_async_copy(v_hbm.at[b, pl.ds(k0, tq), pl.ds(c0, width)], vbuf, sem.at[1])
        copy_k.start()
        copy_v.start()
        copy_k.wait()
        copy_v.wait()
        _sb_update(qs, [kbuf[:, _head(hd)] for hd in range(heads)], [vbuf[:, _head(hd)] for hd in range(heads)],
                   carry_scr, acc_scr, sfx, None, False, cmin_scr)
        return j + 1, cmin_scr[0]

    lax.while_loop(cond, body, (jnp.int32(0), cmin_scr[0]))
    for hd in range(heads):
        o_ref[0, :, _head(hd)] = acc_scr[hd].astype(o_ref.dtype)


def _sb_prompt(sq, sk, sv):
    bsz, t, _ = sq.shape
    blk = SB_QBLOCK
    assert t % blk == 0
    width = SB_GROUP * SB_DH
    sfx = _suffix_matrix(blk)
    own = pl.BlockSpec((1, blk, width), lambda b, g, i: (b, i, g))
    before = pl.BlockSpec((1, blk, width), lambda b, g, i: (b, jnp.maximum(i - 1, 0), g))
    hbm_spec = pl.BlockSpec(memory_space=pl.ANY)
    return pl.pallas_call(
        _sb_prompt_kernel,
        grid=(bsz, SB_HEADS // SB_GROUP, t // blk),
        in_specs=[own, own, own, before, before, hbm_spec, hbm_spec, _const_spec(sfx.shape)],
        out_specs=own,
        out_shape=jax.ShapeDtypeStruct((bsz, t, SB_W), BF16),
        scratch_shapes=[pltpu.VMEM((SB_GROUP, blk, LANES), F32), pltpu.VMEM((SB_GROUP, blk, SB_DH), F32),
                        pltpu.VMEM((blk, width), BF16), pltpu.VMEM((blk, width), BF16),
                        pltpu.SemaphoreType.DMA((2,)), pltpu.SMEM((1,), F32)],
        compiler_params=_params("parallel", "parallel", "arbitrary"),
        name="sb_prompt",
    )(sq, sk, sv, sk, sv, sk, sv, sfx)


def _sb_sample_kernel(q_ref, kn_ref, vn_ref, kw_ref, vw_ref, k_hbm, v_hbm, sfx_ref, o_ref,
                      carry_scr, acc_scr, kbuf, vbuf, sem, cmin_scr, *, n_past):
    b0 = pl.program_id(0) * q_ref.shape[0]
    n_seq, tq, _ = q_ref.shape
    blk = SB_BLOCK
    win = blk - tq
    sfx = sfx_ref[...]
    rows = lax.broadcasted_iota(jnp.int32, (tq, blk), 0)
    cols = lax.broadcasted_iota(jnp.int32, (tq, blk), 1)
    slots = [(si, hd) for si in range(n_seq) for hd in range(SB_HEADS)]

    def head_rows(ref, hd, first, n):
        return ref[pl.ds(first * SB_HEADS + hd, n, stride=SB_HEADS), :].astype(BF16)

    visible = (cols < win) | (cols - win < rows)
    qs = [q_ref[si, :, _head(hd)] for si, hd in slots]
    _sb_update(qs,
               [jnp.concatenate([head_rows(kw_ref.at[si], hd, blk - win, win), kn_ref[si, :, _head(hd)]], axis=0)
                for si, hd in slots],
               [jnp.concatenate([head_rows(vw_ref.at[si], hd, blk - win, win), vn_ref[si, :, _head(hd)]], axis=0)
                for si, hd in slots],
               carry_scr, acc_scr, sfx, (visible,), True, cmin_scr)

    n_left = n_past - win
    n_older = -(-n_left // blk)

    def cond(state):
        j, cmin = state
        return (j < n_older) & (cmin <= SB_DEAD)

    def body(state):
        j, _ = state
        end = n_left - j * blk
        start = jnp.maximum(end - blk, 0)
        row0 = pl.multiple_of(start * SB_HEADS, SB_HEADS)
        copies = [pltpu.make_async_copy(hbm.at[b0 + si, pl.ds(row0, blk * SB_HEADS)], buf.at[si], sem.at[w, si])
                  for si in range(n_seq) for w, (hbm, buf) in enumerate(((k_hbm, kbuf), (v_hbm, vbuf)))]
        for copy in copies:
            copy.start()
        for copy in copies:
            copy.wait()
        unread = cols < end - start
        _sb_update(qs, [head_rows(kbuf.at[si], hd, 0, blk) for si, hd in slots],
                   [head_rows(vbuf.at[si], hd, 0, blk) for si, hd in slots],
                   carry_scr, acc_scr, sfx, (unread,), False, cmin_scr)
        return j + 1, cmin_scr[0]

    lax.while_loop(cond, body, (jnp.int32(0), cmin_scr[0]))
    for slot, (si, hd) in enumerate(slots):
        o_ref[si, :, _head(hd)] = acc_scr[slot].astype(o_ref.dtype)


def _sb_sample(sq, sk, sv, k_past, v_past):
    bsz, t, _ = sq.shape
    p = k_past.shape[1]
    blk = SB_BLOCK
    n_seq = SB_SAMPLE_SEQS if bsz % SB_SAMPLE_SEQS == 0 else 1
    assert p % blk == 0 and p >= blk and t < blk and t % 16 == 0
    k_past = k_past.reshape(bsz, p * SB_HEADS, SB_DH)
    v_past = v_past.reshape(bsz, p * SB_HEADS, SB_DH)
    sfx = _suffix_matrix(blk)
    new_spec = pl.BlockSpec((n_seq, t, SB_W), lambda b: (b, 0, 0))
    window_spec = pl.BlockSpec((n_seq, blk * SB_HEADS, SB_DH), lambda b: (b, p // blk - 1, 0))
    hbm_spec = pl.BlockSpec(memory_space=pl.ANY)
    slots = n_seq * SB_HEADS
    return pl.pallas_call(
        functools.partial(_sb_sample_kernel, n_past=p),
        grid=(bsz // n_seq,),
        in_specs=[new_spec, new_spec, new_spec, window_spec, window_spec, hbm_spec, hbm_spec,
                  _const_spec(sfx.shape)],
        out_specs=new_spec,
        out_shape=jax.ShapeDtypeStruct((bsz, t, SB_W), BF16),
        scratch_shapes=[pltpu.VMEM((slots, t, LANES), F32), pltpu.VMEM((slots, t, SB_DH), F32),
                        pltpu.VMEM((n_seq, blk * SB_HEADS, SB_DH), F32),
                        pltpu.VMEM((n_seq, blk * SB_HEADS, SB_DH), F32),
                        pltpu.SemaphoreType.DMA((2, n_seq)), pltpu.SMEM((1,), F32)],
        compiler_params=_params("arbitrary"),
        name="sb_sample",
    )(sq, sk, sv, k_past, v_past, k_past, v_past, sfx)


def _merge_ffn_kernel(h_ref, oa_ref, ob_ref, ga_ref, gb_ref, wa_ref, wb_ref, wm_ref, g2_ref, b2_ref,
                      wg_ref, wu_ref, wo_ref, g3_ref, b3_ref, o_ref):
    branch_a = _dot(oa_ref[...], wa_ref[...])
    branch_b = _dot(ob_ref[...], wb_ref[...])
    merged = jax.nn.sigmoid(ga_ref[...]) * branch_a + jax.nn.sigmoid(gb_ref[...]) * branch_b
    mix = _dot(merged.astype(BF16), wm_ref[...])
    h2 = _layer_norm(DN_ALPHA * h_ref[...] + mix, g2_ref[...], b2_ref[...])
    o_ref[...] = _swiglu_ln(h2, wg_ref, wu_ref, wo_ref, g3_ref, b3_ref)


def _merge_ffn(groups, consts):
    outs = _rowwise_call(_merge_ffn_kernel, groups, consts, [(D_MODEL, F32)], TAIL_ROW_TILE, "merge_ffn")
    return [o[0] for o in outs]


def _cast_kernel(*refs, plans):
    n_in = len(plans)
    outs = iter(refs[n_in:])
    for ref, plan in zip(refs[:n_in], plans):
        x = ref[...]
        for lo, hi, keep in plan:
            y = x[:, lo:hi]
            if keep < hi - lo:
                y = jnp.where(lax.broadcasted_iota(jnp.int32, y.shape, 1) < keep, y, 0.0)
            next(outs)[...] = y.astype(BF16)


def _cast_weights(arrays, plans, rows_per_step):
    rows = arrays[0].shape[0]
    assert all(a.shape[0] == rows for a in arrays) and rows % rows_per_step == 0
    out_shape = [jax.ShapeDtypeStruct((rows, hi - lo), BF16) for plan in plans for lo, hi, _ in plan]
    return pl.pallas_call(
        functools.partial(_cast_kernel, plans=plans),
        grid=(rows // rows_per_step,),
        in_specs=[pl.BlockSpec((rows_per_step, a.shape[1]), lambda i: (i, 0)) for a in arrays],
        out_specs=[pl.BlockSpec((rows_per_step, o.shape[1]), lambda i: (i, 0)) for o in out_shape],
        out_shape=out_shape,
        compiler_params=_params("parallel"),
        name="cast_weights",
    )(*arrays)


def _row(v):
    return v.reshape(1, -1)


def _layer(xs, s0s, pasts, wts):
    shapes = [x.shape[:2] for x in xs]
    hs = _ffn_ln([x.reshape(-1, D_MODEL) for x in xs], wts["ffn1"])
    projected = _proj(hs, wts["proj"])
    tails, states, new_kv = [], [], []
    for (bsz, t), h, s0, past, outs in zip(shapes, hs, s0s, pasts, projected):
        gq, gk, gv, gr, la, sq, sk, sv, ga, gb, skb, svb = outs

        def seq(a):
            return a.reshape(bsz, t, a.shape[-1])

        o_a, s_new = _gla(seq(gq), seq(gk), seq(gv), seq(la), seq(gr), wts["gla_norm"], s0)
        if past is None:
            o_b = _sb_prompt(seq(sq), seq(skb), seq(svb))
        else:
            o_b = _sb_sample(seq(sq), seq(skb), seq(svb), *past)
        tails.append([h, o_a.reshape(-1, GLA_V), o_b.reshape(-1, SB_W), ga, gb])
        states.append(s_new)
        new_kv.append((sk.reshape(bsz, t, SB_HEADS, SB_DH), sv.reshape(bsz, t, SB_HEADS, SB_DH)))
    ys = _merge_ffn(tails, wts["merge"] + wts["ffn2"])
    return [(y.reshape(bsz, t, D_MODEL), s_new, k_new, v_new)
            for (bsz, t), y, s_new, (k_new, v_new) in zip(shapes, ys, states, new_kv)]


def kernel(x_prompt, x_sample, state_gla, cache_sb_k, cache_sb_v, ffn1_w_in, ffn1_w_out, ln1_g, ln1_b, w_in, w_gla_gate_up, b_gla_gate, g_gla_norm, w_gla_o, w_sb_o, w_out, ln2_g, ln2_b, ffn2_w_in, ffn2_w_out, ln3_g, ln3_b):
    xp, xs = x_prompt, x_sample
    bp = xp.shape[0]
    outs = [[] for _ in range(6)]
    lr_lo = 2 * GLA_QK + 2 * GLA_V
    lr_hi = lr_lo + GLA_GATE_RANK
    whole = ((0, D_MODEL, D_MODEL),)
    halves = ((0, D_FF, D_FF), (D_FF, 2 * D_FF, D_FF))
    for l in range(DEPTH):
        wg1, wu1, wg2, wu2, w_ga, w_sb, w_o = _cast_weights(
            [ffn1_w_in[l], ffn2_w_in[l], w_gla_o[l], w_sb_o[l], w_out[l]],
            (halves, halves, whole, whole, whole), CAST_ROWS)
        w_in_t = jnp.swapaxes(w_in[l], 0, 1)
        w_a = w_in_t[:lr_lo].astype(BF16)
        w_lr = jnp.pad(w_in_t[lr_lo:lr_hi], ((0, LANES - GLA_GATE_RANK), (0, 0))).astype(BF16)
        w_b = w_in_t[lr_hi:].astype(BF16)
        wo1, wo2 = _cast_weights([ffn1_w_out[l], ffn2_w_out[l]], (whole, whole), D_FF // 4)
        w_up = jnp.pad(w_gla_gate_up[l], ((0, LANES - GLA_GATE_RANK), (0, 0))).astype(BF16)
        wts = {
            "ffn1": (wg1, wu1, wo1, _row(ln1_g[l]), _row(ln1_b[l])),
            "proj": (w_a, w_b, w_lr, w_up, _row(b_gla_gate[l])),
            "gla_norm": _row(g_gla_norm[l]),
            "merge": (w_ga, w_sb, w_o, _row(ln2_g[l]), _row(ln2_b[l])),
            "ffn2": (wg2, wu2, wo2, _row(ln3_g[l]), _row(ln3_b[l])),
        }
        s0_p = jnp.zeros((bp, GLA_HEADS, GLA_DK, GLA_DV), F32)
        (xp, sp, kp, vp), (xs, ss, ksn, vsn) = _layer(
            [xp, xs], [s0_p, state_gla[l]], [None, (cache_sb_k[l], cache_sb_v[l])], wts)
        for lst, val in zip(outs, (sp, kp, vp, ss, ksn, vsn)):
            lst.append(val)
    return (xp, xs, *(jnp.stack(o) for o in outs))
```

```python
import functools

import jax
import jax.numpy as jnp
from jax import lax
from jax.experimental import pallas as pl
from jax.experimental.pallas import tpu as pltpu

F32 = jnp.float32
BF16 = jnp.bfloat16

D_MODEL = 1024
CHUNK = 64
GLA_HEADS = 4
GLA_DK = 128
GLA_DV = 256
GLA_QK = GLA_HEADS * GLA_DK
GLA_V = GLA_HEADS * GLA_DV
GLA_GATE_RANK = 16
GLA_GATE_TEMP = 16.0
SB_HEADS = 8
SB_DH = 128
SB_W = SB_HEADS * SB_DH
D_FF = 2816
LN_EPS = 1e-5
DEPTH = 1
DN_ALPHA = (2 * DEPTH) ** 0.25

LOG2E = 1.4426950408889634
LANES = 128
GLA_ROWS = 512
GLA_BATCHED_ROWS = 256
ROW_TILE = 512
PROJ_ROW_TILE = 256
CAST_ROWS = 64
TAIL_ROW_TILE = 256
SB_BLOCK = 256
SB_SAMPLE_SEQS = 2
SB_QBLOCK = 256
SB_GROUP = 8
SB_DEAD = 110.0
VMEM_LIMIT = 56 * 1024 * 1024


def _layer_norm(r, g, b):
    mu = jnp.mean(r, axis=-1, keepdims=True)
    d = r - mu
    var = jnp.mean(d * d, axis=-1, keepdims=True)
    return d * lax.rsqrt(var + LN_EPS) * g + b


def _softplus_neg_abs(z):
    return jnp.log(1.0 + jnp.exp(-jnp.abs(z)))


def _split_hi_lo(x):
    hi = x.astype(BF16)
    return hi, (x - hi.astype(F32)).astype(BF16)


def _dot(a, b):
    return jnp.dot(a, b, preferred_element_type=F32)


def _dot_nt(a, b):
    return lax.dot_general(a, b, (((1,), (1,)), ((), ())), preferred_element_type=F32)


def _dot_tn(a, b):
    return lax.dot_general(a, b, (((0,), (0,)), ((), ())), preferred_element_type=F32)


def _const_spec(shape):
    nd = len(shape)
    return pl.BlockSpec(shape, lambda *_: (0,) * nd, pipeline_mode=pl.Buffered(1))


def _params(*sem):
    return pltpu.CompilerParams(dimension_semantics=sem, vmem_limit_bytes=VMEM_LIMIT)


def _rowwise_call(body, row_inputs, consts, out_cols, tm, name):
    n_in, n_out, n_groups = len(row_inputs[0]), len(out_cols), len(row_inputs)
    rows = [g[0].shape[0] for g in row_inputs]
    tiles = [min(tm, r) for r in rows]
    steps = [r // t for r, t in zip(rows, tiles)]
    assert all(r % t == 0 for r, t in zip(rows, tiles))
    starts = [sum(steps[:g]) for g in range(n_groups)]

    def spec(g, width):
        return pl.BlockSpec((tiles[g], width), lambda i: (jnp.clip(i - starts[g], 0, steps[g] - 1), 0))

    def kern(*refs):
        i = pl.program_id(0)
        c0 = n_groups * n_in
        o0 = c0 + len(consts)
        for g in range(n_groups):
            @pl.when((i >= starts[g]) & (i < starts[g] + steps[g]))
            def _(g=g):
                body(*refs[g * n_in:(g + 1) * n_in], *refs[c0:o0], *refs[o0 + g * n_out:o0 + (g + 1) * n_out])

    outs = pl.pallas_call(
        kern,
        grid=(sum(steps),),
        in_specs=([spec(g, a.shape[1]) for g in range(n_groups) for a in row_inputs[g]]
                  + [_const_spec(c.shape) for c in consts]),
        out_specs=[spec(g, w) for g in range(n_groups) for w, _ in out_cols],
        out_shape=[jax.ShapeDtypeStruct((rows[g], w), d) for g in range(n_groups) for w, d in out_cols],
        compiler_params=_params("arbitrary"),
        name=name,
    )(*[a for g in row_inputs for a in g], *consts)
    return [outs[g * n_out:(g + 1) * n_out] for g in range(n_groups)]


def _swiglu_ln(x, wg_ref, wu_ref, wo_ref, g_ref, b_ref):
    xb = x.astype(BF16)
    gate = _dot(xb, wg_ref[...])
    up = _dot(xb, wu_ref[...])
    a = (gate * jax.nn.sigmoid(gate) * up).astype(BF16)
    return _layer_norm(DN_ALPHA * x + 0.5 * _dot(a, wo_ref[...]), g_ref[...], b_ref[...])


def _ffn_ln_kernel(x_ref, wg_ref, wu_ref, wo_ref, g_ref, b_ref, o_ref):
    o_ref[...] = _swiglu_ln(x_ref[...], wg_ref, wu_ref, wo_ref, g_ref, b_ref)


def _ffn_ln(xs, consts):
    outs = _rowwise_call(_ffn_ln_kernel, [[x] for x in xs], consts, [(D_MODEL, F32)], ROW_TILE, "ffn_ln")
    return [o[0] for o in outs]


_PROJ_GROUPS_A = (("gq", GLA_QK), ("gk", GLA_QK), ("gv", GLA_V), ("gr", GLA_V))
_PROJ_GROUPS_B = (("sq", SB_W), ("sk", SB_W), ("sv", SB_W), ("ga", D_MODEL), ("gb", D_MODEL))


def _proj_kernel(h_ref, wa_ref, wb_ref, wlr_ref, wup_ref, bup_ref,
                 gq_ref, gk_ref, gv_ref, gr_ref, la_ref, sq_ref, sk_ref, sv_ref, ga_ref, gb_ref,
                 skb_ref, svb_ref):
    hb = h_ref[...].astype(BF16)
    outs = dict(gq=gq_ref, gk=gk_ref, gv=gv_ref, gr=gr_ref, sq=sq_ref, sk=sk_ref, sv=sv_ref,
                ga=ga_ref, gb=gb_ref)
    for w_ref, groups in ((wa_ref, _PROJ_GROUPS_A), (wb_ref, _PROJ_GROUPS_B)):
        lo = 0
        for name, width in groups:
            y = _dot_nt(hb, w_ref[lo:lo + width, :])
            lo += width
            if name == "gq":
                y = y * (GLA_DK ** -0.5)
            if name == "sq":
                y = y * (SB_DH ** -0.5)
            if name in ("sq", "gv"):
                outs[name][...] = y.astype(BF16)
            else:
                outs[name][...] = y
            if name == "sk":
                skb_ref[...] = y.astype(BF16)
            if name == "sv":
                svb_ref[...] = y.astype(BF16)
    lr = _dot_nt(hb, wlr_ref[...]).astype(BF16)
    pre = _dot(lr, wup_ref[...]) + bup_ref[...]
    la_ref[...] = (jnp.minimum(pre, 0.0) - _softplus_neg_abs(pre)) * (1.0 / GLA_GATE_TEMP)


def _proj(hs, consts):
    widths = (GLA_QK, GLA_QK, GLA_V, GLA_V, GLA_QK, SB_W, SB_W, SB_W, D_MODEL, D_MODEL, SB_W, SB_W)
    dtypes = (F32, F32, BF16, F32, F32, BF16, F32, F32, F32, F32, BF16, BF16)
    return _rowwise_call(_proj_kernel, [[h] for h in hs], consts, list(zip(widths, dtypes)), PROJ_ROW_TILE, "proj")


def _gla_cumsum_matrix(tb):
    r = lax.broadcasted_iota(jnp.int32, (2 * tb, tb), 0)
    c = lax.broadcasted_iota(jnp.int32, (2 * tb, tb), 1)
    same_chunk = (r % tb) // CHUNK == c // CHUNK
    return (same_chunk & ((c <= r) | (r >= tb))).astype(BF16)


def _gla_kernel(q_ref, k_ref, v_ref, la_ref, gr_ref, gn_ref, s0_ref, cm_ref, o_ref, sout_ref, s_scr):
    t = pl.program_id(1)

    @pl.when(t == 0)
    def _():
        s_scr[...] = s0_ref[...]

    n_rows = q_ref.shape[0]
    n_seq = s_scr.shape[0]
    chunks_per_seq = n_rows // CHUNK // n_seq
    sb = cm_ref.shape[1]
    cm = cm_ref[...]
    gn = gn_ref[...]
    rows = lax.broadcasted_iota(jnp.int32, (sb, sb), 0)
    cols = lax.broadcasted_iota(jnp.int32, (sb, sb), 1)
    shift = CHUNK.bit_length() - 1
    causal = (cols <= rows) & ((rows >> shift) == (cols >> shift))

    heads = range(GLA_HEADS)
    ks = [slice(h * GLA_DK, (h + 1) * GLA_DK) for h in heads]
    vs = [slice(h * GLA_DV, (h + 1) * GLA_DV) for h in heads]
    for r0 in range(0, n_rows, sb):
        rb = slice(r0, r0 + sb)
        cs = [_dot(cm, jnp.concatenate(_split_hi_lo(la_ref[rb, ks[h]]), axis=1)) for h in heads]
        cs = [c[:, :GLA_DK] + c[:, GLA_DK:] for c in cs]
        b = [c[:sb] for c in cs]
        b_tot = [c[sb:] for c in cs]
        v = [v_ref[rb, vs[h]] for h in heads]
        qg = [(q_ref[rb, ks[h]] * jnp.exp(b[h])).astype(BF16) for h in heads]
        kg = [(k_ref[rb, ks[h]] * jnp.exp(-b[h])).astype(BF16) for h in heads]
        kd = [(k_ref[rb, ks[h]] * jnp.exp(b_tot[h] - b[h])).astype(BF16) for h in heads]
        att = [jnp.where(causal, _dot_nt(qg[h], kg[h]), 0.0).astype(BF16) for h in heads]
        o_intra = [_dot(att[h], v[h]) for h in heads]
        s = [None] * GLA_HEADS
        for c in range(sb // CHUNK):
            seq = (r0 // CHUNK + c) // chunks_per_seq
            first_of_run = c == 0 or (r0 // CHUNK + c) % chunks_per_seq == 0
            last_of_run = c == sb // CHUNK - 1 or (r0 // CHUNK + c + 1) % chunks_per_seq == 0
            rs = slice(c * CHUNK, (c + 1) * CHUNK)
            ro = slice(r0 + c * CHUNK, r0 + (c + 1) * CHUNK)
            for h in heads:
                if first_of_run:
                    s[h] = s_scr[seq, h]
                o = o_intra[h][rs] + _dot(qg[h][rs], s[h].astype(BF16))
                o = o * lax.rsqrt(jnp.mean(o * o, axis=-1, keepdims=True) + LN_EPS) * gn[:, vs[h]]
                gate = gr_ref[ro, vs[h]]
                o_ref[ro, vs[h]] = (o * (gate * jax.nn.sigmoid(gate))).astype(BF16)
                decay = jnp.exp(jnp.broadcast_to(b_tot[h][c * CHUNK:c * CHUNK + 1], (GLA_DK, GLA_DK))).T
                s[h] = jnp.concatenate([decay, decay], axis=1) * s[h] + _dot_tn(kd[h][rs], v[h][rs])
                if last_of_run:
                    s_scr[seq, h] = s[h]

    @pl.when(t == pl.num_programs(1) - 1)
    def _():
        sout_ref[...] = s_scr[...]


def _gla(q, k, v, la, gr, gn, s0):
    bsz, t, _ = q.shape
    n_rows = min(GLA_ROWS, bsz * t)
    n_seq = max(1, n_rows // t)
    n_inner = max(1, t // n_rows)
    assert (bsz * t) % n_rows == 0 and bsz % n_seq == 0 and t % CHUNK == 0 and (t % n_rows == 0 or n_rows % t == 0)
    cm = _gla_cumsum_matrix(min(n_rows, GLA_BATCHED_ROWS))

    def rows(width):
        return pl.BlockSpec((n_rows, width), lambda b, i: (b * n_inner + i, 0))

    def flat(a):
        return a.reshape(bsz * t, a.shape[-1])

    s_spec = pl.BlockSpec((n_seq, GLA_HEADS, GLA_DK, GLA_DV), lambda b, i: (b, 0, 0, 0))
    o, s_new = pl.pallas_call(
        _gla_kernel,
        grid=(bsz // n_seq, n_inner),
        in_specs=[rows(GLA_QK), rows(GLA_QK), rows(GLA_V), rows(GLA_QK), rows(GLA_V), _const_spec(gn.shape),
                  s_spec, _const_spec(cm.shape)],
        out_specs=[rows(GLA_V), s_spec],
        out_shape=[jax.ShapeDtypeStruct((bsz * t, GLA_V), BF16),
                   jax.ShapeDtypeStruct((bsz, GLA_HEADS, GLA_DK, GLA_DV), F32)],
        scratch_shapes=[pltpu.VMEM((n_seq, GLA_HEADS, GLA_DK, GLA_DV), F32)],
        compiler_params=_params("parallel", "arbitrary"),
        name="gla",
    )(flat(q), flat(k), flat(v), flat(la), flat(gr), gn, s0, cm)
    return o, s_new


def _suffix_matrix(tk):
    r = lax.broadcasted_iota(jnp.int32, (2 * tk, tk), 0) % tk
    c = lax.broadcasted_iota(jnp.int32, (2 * tk, tk), 1)
    return (r >= c).astype(BF16)


def _sb_update(qs, ks, vs, carry_ref, acc_ref, suffix, visible, first, cmin_ref=None):
    heads = range(len(qs))
    sub = suffix.shape[1]
    n_sub = ks[0].shape[0] // sub
    z = [_dot_nt(qs[hd], ks[hd]) for hd in heads]
    carry = [None if first else carry_ref[hd] for hd in heads]
    ws = [[None] * n_sub for _ in heads]
    for s in reversed(range(n_sub)):
        vis = None if visible is None else visible[s]
        zs = [z[hd][:, s * sub:(s + 1) * sub] for hd in heads]
        if vis is not None:
            zs = [jnp.where(vis, x, -jnp.inf) for x in zs]
        cs = [jnp.maximum(x, 0.0) + jnp.log(1.0 + jnp.exp2(jnp.abs(x) * -LOG2E)) for x in zs]
        sums = [_dot(jnp.concatenate(_split_hi_lo(c), axis=1), suffix) for c in cs]
        for hd in heads:
            e = zs[hd] - sums[hd]
            if carry[hd] is not None:
                e = e - jnp.concatenate([carry[hd]] * (sub // LANES), axis=1)
            ws[hd][s] = jnp.exp(e).astype(BF16)
            total = jnp.broadcast_to(jnp.sum(cs[hd], axis=-1, keepdims=True), (cs[hd].shape[0], LANES))
            carry[hd] = total if carry[hd] is None else carry[hd] + total
    if cmin_ref is not None:
        cmin_ref[0] = jnp.min(functools.reduce(jnp.minimum, carry))
    for hd in heads:
        pv = _dot(ws[hd][0] if n_sub == 1 else jnp.concatenate(ws[hd], axis=1), vs[hd])
        acc_ref[hd] = pv if first else acc_ref[hd] + pv
        carry_ref[hd] = carry[hd]


def _head(hd):
    return slice(hd * SB_DH, (hd + 1) * SB_DH)


def _sb_prompt_kernel(q_ref, kd_ref, vd_ref, kp_ref, vp_ref, k_hbm, v_hbm, sfx_ref, o_ref,
                      carry_scr, acc_scr, kbuf, vbuf, sem, cmin_scr):
    b, g, i = pl.program_id(0), pl.program_id(1), pl.program_id(2)
    tq = q_ref.shape[1]
    width = q_ref.shape[2]
    heads = width // SB_DH
    sfx = sfx_ref[...]
    rows = lax.broadcasted_iota(jnp.int32, (tq, tq), 0)
    cols = lax.broadcasted_iota(jnp.int32, (tq, tq), 1)
    strictly_earlier = cols < rows
    qs = [q_ref[0, :, _head(hd)] for hd in range(heads)]

    @pl.when(i == 0)
    def _():
        _sb_update(qs, [kd_ref[0, :, _head(hd)] for hd in range(heads)],
                   [vd_ref[0, :, _head(hd)] for hd in range(heads)],
                   carry_scr, acc_scr, sfx, (strictly_earlier,), True, cmin_scr)

    @pl.when(i > 0)
    def _():
        _sb_update(qs, [jnp.concatenate([kp_ref[0, :, _head(hd)], kd_ref[0, :, _head(hd)]], axis=0)
                        for hd in range(heads)],
                   [jnp.concatenate([vp_ref[0, :, _head(hd)], vd_ref[0, :, _head(hd)]], axis=0)
                    for hd in range(heads)],
                   carry_scr, acc_scr, sfx, (None, strictly_earlier), True, cmin_scr)

    n_older = jnp.maximum(i - 1, 0)

    def cond(state):
        j, cmin = state
        return (j < n_older) & (cmin <= SB_DEAD)

    def body(state):
        j, _ = state
        k0 = pl.multiple_of((i - 2 - j) * tq, tq)
        c0 = pl.multiple_of(g * width, LANES)
        copy_k = pltpu.make_async_copy(k_hbm.at[b, pl.ds(k0, tq), pl.ds(c0, width)], kbuf, sem.at[0])
        copy_v = pltpu.make_async_copy(v_hbm.at[b, pl.ds(k0, tq), pl.ds(c0, width)], vbuf, sem.at[1])
        copy_k.start()
        copy_v.start()
        copy_k.wait()
        copy_v.wait()
        _sb_update(qs, [kbuf[:, _head(hd)] for hd in range(heads)], [vbuf[:, _head(hd)] for hd in range(heads)],
                   carry_scr, acc_scr, sfx, None, False, cmin_scr)
        return j + 1, cmin_scr[0]

    lax.while_loop(cond, body, (jnp.int32(0), cmin_scr[0]))
    for hd in range(heads):
        o_ref[0, :, _head(hd)] = acc_scr[hd].astype(o_ref.dtype)


def _sb_prompt(sq, sk, sv):
    bsz, t, _ = sq.shape
    blk = SB_QBLOCK
    assert t % blk == 0
    width = SB_GROUP * SB_DH
    sfx = _suffix_matrix(blk)
    own = pl.BlockSpec((1, blk, width), lambda b, g, i: (b, i, g))
    before = pl.BlockSpec((1, blk, width), lambda b, g, i: (b, jnp.maximum(i - 1, 0), g))
    hbm_spec = pl.BlockSpec(memory_space=pl.ANY)
    return pl.pallas_call(
        _sb_prompt_kernel,
        grid=(bsz, SB_HEADS // SB_GROUP, t // blk),
        in_specs=[own, own, own, before, before, hbm_spec, hbm_spec, _const_spec(sfx.shape)],
        out_specs=own,
        out_shape=jax.ShapeDtypeStruct((bsz, t, SB_W), BF16),
        scratch_shapes=[pltpu.VMEM((SB_GROUP, blk, LANES), F32), pltpu.VMEM((SB_GROUP, blk, SB_DH), F32),
                        pltpu.VMEM((blk, width), BF16), pltpu.VMEM((blk, width), BF16),
                        pltpu.SemaphoreType.DMA((2,)), pltpu.SMEM((1,), F32)],
        compiler_params=_params("parallel", "parallel", "arbitrary"),
        name="sb_prompt",
    )(sq, sk, sv, sk, sv, sk, sv, sfx)


def _sb_sample_kernel(q_ref, kn_ref, vn_ref, kw_ref, vw_ref, k_hbm, v_hbm, sfx_ref, o_ref,
                      carry_scr, acc_scr, kbuf, vbuf, sem, cmin_scr, *, n_past):
    b0 = pl.program_id(0) * q_ref.shape[0]
    n_seq, tq, _ = q_ref.shape
    blk = SB_BLOCK
    win = blk - tq
    sfx = sfx_ref[...]
    rows = lax.broadcasted_iota(jnp.int32, (tq, blk), 0)
    cols = lax.broadcasted_iota(jnp.int32, (tq, blk), 1)
    slots = [(si, hd) for si in range(n_seq) for hd in range(SB_HEADS)]

    def head_rows(ref, hd, first, n):
        return ref[pl.ds(first * SB_HEADS + hd, n, stride=SB_HEADS), :].astype(BF16)

    visible = (cols < win) | (cols - win < rows)
    qs = [q_ref[si, :, _head(hd)] for si, hd in slots]
    _sb_update(qs,
               [jnp.concatenate([head_rows(kw_ref.at[si], hd, blk - win, win), kn_ref[si, :, _head(hd)]], axis=0)
                for si, hd in slots],
               [jnp.concatenate([head_rows(vw_ref.at[si], hd, blk - win, win), vn_ref[si, :, _head(hd)]], axis=0)
                for si, hd in slots],
               carry_scr, acc_scr, sfx, (visible,), True, cmin_scr)

    n_left = n_past - win
    n_older = -(-n_left // blk)

    def cond(state):
        j, cmin = state
        return (j < n_older) & (cmin <= SB_DEAD)

    def body(state):
        j, _ = state
        end = n_left - j * blk
        start = jnp.maximum(end - blk, 0)
        row0 = pl.multiple_of(start * SB_HEADS, SB_HEADS)
        copies = [pltpu.make_async_copy(hbm.at[b0 + si, pl.ds(row0, blk * SB_HEADS)], buf.at[si], sem.at[w, si])
                  for si in range(n_seq) for w, (hbm, buf) in enumerate(((k_hbm, kbuf), (v_hbm, vbuf)))]
        for copy in copies:
            copy.start()
        for copy in copies:
            copy.wait()
        unread = cols < end - start
        _sb_update(qs, [head_rows(kbuf.at[si], hd, 0, blk) for si, hd in slots],
                   [head_rows(vbuf.at[si], hd, 0, blk) for si, hd in slots],
                   carry_scr, acc_scr, sfx, (unread,), False, cmin_scr)
        return j + 1, cmin_scr[0]

    lax.while_loop(cond, body, (jnp.int32(0), cmin_scr[0]))
    for slot, (si, hd) in enumerate(slots):
        o_ref[si, :, _head(hd)] = acc_scr[slot].astype(o_ref.dtype)


def _sb_sample(sq, sk, sv, k_past, v_past):
    bsz, t, _ = sq.shape
    p = k_past.shape[1]
    blk = SB_BLOCK
    n_seq = SB_SAMPLE_SEQS if bsz % SB_SAMPLE_SEQS == 0 else 1
    assert p % blk == 0 and p >= blk and t < blk and t % 16 == 0
    k_past = k_past.reshape(bsz, p * SB_HEADS, SB_DH)
    v_past = v_past.reshape(bsz, p * SB_HEADS, SB_DH)
    sfx = _suffix_matrix(blk)
    new_spec = pl.BlockSpec((n_seq, t, SB_W), lambda b: (b, 0, 0))
    window_spec = pl.BlockSpec((n_seq, blk * SB_HEADS, SB_DH), lambda b: (b, p // blk - 1, 0))
    hbm_spec = pl.BlockSpec(memory_space=pl.ANY)
    slots = n_seq * SB_HEADS
    return pl.pallas_call(
        functools.partial(_sb_sample_kernel, n_past=p),
        grid=(bsz // n_seq,),
        in_specs=[new_spec, new_spec, new_spec, window_spec, window_spec, hbm_spec, hbm_spec,
                  _const_spec(sfx.shape)],
        out_specs=new_spec,
        out_shape=jax.ShapeDtypeStruct((bsz, t, SB_W), BF16),
        scratch_shapes=[pltpu.VMEM((slots, t, LANES), F32), pltpu.VMEM((slots, t, SB_DH), F32),
                        pltpu.VMEM((n_seq, blk * SB_HEADS, SB_DH), F32),
                        pltpu.VMEM((n_seq, blk * SB_HEADS, SB_DH), F32),
                        pltpu.SemaphoreType.DMA((2, n_seq)), pltpu.SMEM((1,), F32)],
        compiler_params=_params("arbitrary"),
        name="sb_sample",
    )(sq, sk, sv, k_past, v_past, k_past, v_past, sfx)


def _merge_ffn_kernel(h_ref, oa_ref, ob_ref, ga_ref, gb_ref, wa_ref, wb_ref, wm_ref, g2_ref, b2_ref,
                      wg_ref, wu_ref, wo_ref, g3_ref, b3_ref, o_ref):
    branch_a = _dot(oa_ref[...], wa_ref[...])
    branch_b = _dot(ob_ref[...], wb_ref[...])
    merged = jax.nn.sigmoid(ga_ref[...]) * branch_a + jax.nn.sigmoid(gb_ref[...]) * branch_b
    mix = _dot(merged.astype(BF16), wm_ref[...])
    h2 = _layer_norm(DN_ALPHA * h_ref[...] + mix, g2_ref[...], b2_ref[...])
    o_ref[...] = _swiglu_ln(h2, wg_ref, wu_ref, wo_ref, g3_ref, b3_ref)


def _merge_ffn(groups, consts):
    outs = _rowwise_call(_merge_ffn_kernel, groups, consts, [(D_MODEL, F32)], TAIL_ROW_TILE, "merge_ffn")
    return [o[0] for o in outs]


def _cast_kernel(*refs, plans):
    n_in = len(plans)
    outs = iter(refs[n_in:])
    for ref, plan in zip(refs[:n_in], plans):
        x = ref[...]
        for lo, hi, keep in plan:
            y = x[:, lo:hi]
            if keep < hi - lo:
                y = jnp.where(lax.broadcasted_iota(jnp.int32, y.shape, 1) < keep, y, 0.0)
            next(outs)[...] = y.astype(BF16)


def _cast_weights(arrays, plans, rows_per_step):
    rows = arrays[0].shape[0]
    assert all(a.shape[0] == rows for a in arrays) and rows % rows_per_step == 0
    out_shape = [jax.ShapeDtypeStruct((rows, hi - lo), BF16) for plan in plans for lo, hi, _ in plan]
    return pl.pallas_call(
        functools.partial(_cast_kernel, plans=plans),
        grid=(rows // rows_per_step,),
        in_specs=[pl.BlockSpec((rows_per_step, a.shape[1]), lambda i: (i, 0)) for a in arrays],
        out_specs=[pl.BlockSpec((rows_per_step, o.shape[1]), lambda i: (i, 0)) for o in out_shape],
        out_shape=out_shape,
        compiler_params=_params("parallel"),
        name="cast_weights",
    )(*arrays)


def _row(v):
    return v.reshape(1, -1)


def _layer(xs, s0s, pasts, wts):
    shapes = [x.shape[:2] for x in xs]
    hs = _ffn_ln([x.reshape(-1, D_MODEL) for x in xs], wts["ffn1"])
    projected = _proj(hs, wts["proj"])
    tails, states, new_kv = [], [], []
    for (bsz, t), h, s0, past, outs in zip(shapes, hs, s0s, pasts, projected):
        gq, gk, gv, gr, la, sq, sk, sv, ga, gb, skb, svb = outs

        def seq(a):
            return a.reshape(bsz, t, a.shape[-1])

        o_a, s_new = _gla(seq(gq), seq(gk), seq(gv), seq(la), seq(gr), wts["gla_norm"], s0)
        if past is None:
            o_b = _sb_prompt(seq(sq), seq(skb), seq(svb))
        else:
            o_b = _sb_sample(seq(sq), seq(skb), seq(svb), *past)
        tails.append([h, o_a.reshape(-1, GLA_V), o_b.reshape(-1, SB_W), ga, gb])
        states.append(s_new)
        new_kv.append((sk.reshape(bsz, t, SB_HEADS, SB_DH), sv.reshape(bsz, t, SB_HEADS, SB_DH)))
    ys = _merge_ffn(tails, wts["merge"] + wts["ffn2"])
    return [(y.reshape(bsz, t, D_MODEL), s_new, k_new, v_new)
            for (bsz, t), y, s_new, (k_new, v_new) in zip(shapes, ys, states, new_kv)]


def kernel(x_prompt, x_sample, state_gla, cache_sb_k, cache_sb_v, ffn1_w_in, ffn1_w_out, ln1_g, ln1_b, w_in, w_gla_gate_up, b_gla_gate, g_gla_norm, w_gla_o, w_sb_o, w_out, ln2_g, ln2_b, ffn2_w_in, ffn2_w_out, ln3_g, ln3_b):
    xp, xs = x_prompt, x_sample
    bp = xp.shape[0]
    outs = [[] for _ in range(6)]
    lr_lo = 2 * GLA_QK + 2 * GLA_V
    lr_hi = lr_lo + GLA_GATE_RANK
    whole = ((0, D_MODEL, D_MODEL),)
    halves = ((0, D_FF, D_FF), (D_FF, 2 * D_FF, D_FF))
    for l in range(DEPTH):
        wg1, wu1, wg2, wu2, w_ga, w_sb, w_o = _cast_weights(
            [ffn1_w_in[l], ffn2_w_in[l], w_gla_o[l], w_sb_o[l], w_out[l]],
            (halves, halves, whole, whole, whole), CAST_ROWS)
        w_in_t = jnp.swapaxes(w_in[l], 0, 1)
        w_a = w_in_t[:lr_lo].astype(BF16)
        w_lr = jnp.pad(w_in_t[lr_lo:lr_hi], ((0, LANES - GLA_GATE_RANK), (0, 0))).astype(BF16)
        w_b = w_in_t[lr_hi:].astype(BF16)
        wo1, wo2 = _cast_weights([ffn1_w_out[l], ffn2_w_out[l]], (whole, whole), D_FF // 4)
        w_up = jnp.pad(w_gla_gate_up[l], ((0, LANES - GLA_GATE_RANK), (0, 0))).astype(BF16)
        wts = {
            "ffn1": (wg1, wu1, wo1, _row(ln1_g[l]), _row(ln1_b[l])),
            "proj": (w_a, w_b, w_lr, w_up, _row(b_gla_gate[l])),
            "gla_norm": _row(g_gla_norm[l]),
            "merge": (w_ga, w_sb, w_o, _row(ln2_g[l]), _row(ln2_b[l])),
            "ffn2": (wg2, wu2, wo2, _row(ln3_g[l]), _row(ln3_b[l])),
        }
        s0_p = jnp.zeros((bp, GLA_HEADS, GLA_DK, GLA_DV), F32)
        (xp, sp, kp, vp), (xs, ss, ksn, vsn) = _layer(
            [xp, xs], [s0_p, state_gla[l]], [None, (cache_sb_k[l], cache_sb_v[l])], wts)
        for lst, val in zip(outs, (sp, kp, vp, ss, ksn, vsn)):
            lst.append(val)
    return (xp, xs, *(jnp.stack(o) for o in outs))
```

```python
import functools

import jax
import jax.numpy as jnp
from jax import lax
from jax.experimental import pallas as pl
from jax.experimental.pallas import tpu as pltpu

F32 = jnp.float32
BF16 = jnp.bfloat16

D_MODEL = 1024
CHUNK = 64
GLA_HEADS = 4
GLA_DK = 128
GLA_DV = 256
GLA_QK = GLA_HEADS * GLA_DK
GLA_V = GLA_HEADS * GLA_DV
GLA_GATE_RANK = 16
GLA_GATE_TEMP = 16.0
SB_HEADS = 8
SB_DH = 128
SB_W = SB_HEADS * SB_DH
D_FF = 2816
LN_EPS = 1e-5
DEPTH = 1
DN_ALPHA = (2 * DEPTH) ** 0.25

LOG2E = 1.4426950408889634
LANES = 128
GLA_ROWS = 512
GLA_BATCHED_ROWS = 256
ROW_TILE = 512
PROJ_ROW_TILE = 256
CAST_ROWS = 64
TAIL_ROW_TILE = 256
SB_BLOCK = 256
SB_SAMPLE_SEQS = 2
SB_QBLOCK = 256
SB_GROUP = 8
SB_STAGE_HEADS = 4
SB_DEAD = 110.0
VMEM_LIMIT = 56 * 1024 * 1024


def _layer_norm(r, g, b):
    mu = jnp.mean(r, axis=-1, keepdims=True)
    d = r - mu
    var = jnp.mean(d * d, axis=-1, keepdims=True)
    return d * lax.rsqrt(var + LN_EPS) * g + b


def _softplus_neg_abs(z):
    return jnp.log(1.0 + jnp.exp(-jnp.abs(z)))


def _split_hi_lo(x):
    hi = x.astype(BF16)
    return hi, (x - hi.astype(F32)).astype(BF16)


def _dot(a, b):
    return jnp.dot(a, b, preferred_element_type=F32)


def _dot_nt(a, b):
    return lax.dot_general(a, b, (((1,), (1,)), ((), ())), preferred_element_type=F32)


def _dot_tn(a, b):
    return lax.dot_general(a, b, (((0,), (0,)), ((), ())), preferred_element_type=F32)


def _const_spec(shape):
    nd = len(shape)
    return pl.BlockSpec(shape, lambda *_: (0,) * nd, pipeline_mode=pl.Buffered(1))


def _params(*sem):
    return pltpu.CompilerParams(dimension_semantics=sem, vmem_limit_bytes=VMEM_LIMIT)


def _rowwise_call(body, row_inputs, consts, out_cols, tm, name):
    n_in, n_out, n_groups = len(row_inputs[0]), len(out_cols), len(row_inputs)
    rows = [g[0].shape[0] for g in row_inputs]
    tiles = [min(tm, r) for r in rows]
    steps = [r // t for r, t in zip(rows, tiles)]
    assert all(r % t == 0 for r, t in zip(rows, tiles))
    starts = [sum(steps[:g]) for g in range(n_groups)]

    def spec(g, width):
        return pl.BlockSpec((tiles[g], width), lambda i: (jnp.clip(i - starts[g], 0, steps[g] - 1), 0))

    def kern(*refs):
        i = pl.program_id(0)
        c0 = n_groups * n_in
        o0 = c0 + len(consts)
        for g in range(n_groups):
            @pl.when((i >= starts[g]) & (i < starts[g] + steps[g]))
            def _(g=g):
                body(*refs[g * n_in:(g + 1) * n_in], *refs[c0:o0], *refs[o0 + g * n_out:o0 + (g + 1) * n_out])

    outs = pl.pallas_call(
        kern,
        grid=(sum(steps),),
        in_specs=([spec(g, a.shape[1]) for g in range(n_groups) for a in row_inputs[g]]
                  + [_const_spec(c.shape) for c in consts]),
        out_specs=[spec(g, w) for g in range(n_groups) for w, _ in out_cols],
        out_shape=[jax.ShapeDtypeStruct((rows[g], w), d) for g in range(n_groups) for w, d in out_cols],
        compiler_params=_params("arbitrary"),
        name=name,
    )(*[a for g in row_inputs for a in g], *consts)
    return [outs[g * n_out:(g + 1) * n_out] for g in range(n_groups)]


def _swiglu_ln(x, wg_ref, wu_ref, wo_ref, g_ref, b_ref):
    xb = x.astype(BF16)
    gate = _dot(xb, wg_ref[...])
    up = _dot(xb, wu_ref[...])
    a = (gate * jax.nn.sigmoid(gate) * up).astype(BF16)
    return _layer_norm(DN_ALPHA * x + 0.5 * _dot(a, wo_ref[...]), g_ref[...], b_ref[...])


def _ffn_ln_kernel(x_ref, wg_ref, wu_ref, wo_ref, g_ref, b_ref, o_ref):
    o_ref[...] = _swiglu_ln(x_ref[...], wg_ref, wu_ref, wo_ref, g_ref, b_ref)


def _ffn_ln(xs, consts):
    outs = _rowwise_call(_ffn_ln_kernel, [[x] for x in xs], consts, [(D_MODEL, F32)], ROW_TILE, "ffn_ln")
    return [o[0] for o in outs]


_PROJ_GROUPS_A = (("gq", GLA_QK), ("gk", GLA_QK), ("gv", GLA_V), ("gr", GLA_V))
_PROJ_GROUPS_B = (("sq", SB_W), ("sk", SB_W), ("sv", SB_W), ("ga", D_MODEL), ("gb", D_MODEL))


def _proj_kernel(h_ref, wa_ref, wb_ref, wlr_ref, wup_ref, bup_ref,
                 gq_ref, gk_ref, gv_ref, gr_ref, la_ref, sq_ref, sk_ref, sv_ref, ga_ref, gb_ref,
                 skb_ref, svb_ref):
    hb = h_ref[...].astype(BF16)
    outs = dict(gq=gq_ref, gk=gk_ref, gv=gv_ref, gr=gr_ref, sq=sq_ref, sk=sk_ref, sv=sv_ref,
                ga=ga_ref, gb=gb_ref)
    for w_ref, groups in ((wa_ref, _PROJ_GROUPS_A), (wb_ref, _PROJ_GROUPS_B)):
        lo = 0
        for name, width in groups:
            y = _dot_nt(hb, w_ref[lo:lo + width, :])
            lo += width
            if name == "gq":
                y = y * (GLA_DK ** -0.5)
            if name == "sq":
                y = y * (SB_DH ** -0.5)
            if name in ("sq", "gv"):
                outs[name][...] = y.astype(BF16)
            else:
                outs[name][...] = y
            if name == "sk":
                skb_ref[...] = y.astype(BF16)
            if name == "sv":
                svb_ref[...] = y.astype(BF16)
    lr = _dot_nt(hb, wlr_ref[...]).astype(BF16)
    pre = _dot(lr, wup_ref[...]) + bup_ref[...]
    la_ref[...] = (jnp.minimum(pre, 0.0) - _softplus_neg_abs(pre)) * (1.0 / GLA_GATE_TEMP)


def _proj(hs, consts):
    widths = (GLA_QK, GLA_QK, GLA_V, GLA_V, GLA_QK, SB_W, SB_W, SB_W, D_MODEL, D_MODEL, SB_W, SB_W)
    dtypes = (F32, F32, BF16, F32, F32, BF16, F32, F32, F32, F32, BF16, BF16)
    return _rowwise_call(_proj_kernel, [[h] for h in hs], consts, list(zip(widths, dtypes)), PROJ_ROW_TILE, "proj")


def _gla_cumsum_matrix(tb):
    r = lax.broadcasted_iota(jnp.int32, (2 * tb, tb), 0)
    c = lax.broadcasted_iota(jnp.int32, (2 * tb, tb), 1)
    same_chunk = (r % tb) // CHUNK == c // CHUNK
    return (same_chunk & ((c <= r) | (r >= tb))).astype(BF16)


def _gla_kernel(q_ref, k_ref, v_ref, la_ref, gr_ref, gn_ref, s0_ref, cm_ref, o_ref, sout_ref, s_scr):
    t = pl.program_id(1)

    @pl.when(t == 0)
    def _():
        s_scr[...] = s0_ref[...]

    n_rows = q_ref.shape[0]
    n_seq = s_scr.shape[0]
    chunks_per_seq = n_rows // CHUNK // n_seq
    sb = cm_ref.shape[1]
    cm = cm_ref[...]
    gn = gn_ref[...]
    rows = lax.broadcasted_iota(jnp.int32, (sb, sb), 0)
    cols = lax.broadcasted_iota(jnp.int32, (sb, sb), 1)
    shift = CHUNK.bit_length() - 1
    causal = (cols <= rows) & ((rows >> shift) == (cols >> shift))

    heads = range(GLA_HEADS)
    ks = [slice(h * GLA_DK, (h + 1) * GLA_DK) for h in heads]
    vs = [slice(h * GLA_DV, (h + 1) * GLA_DV) for h in heads]
    for r0 in range(0, n_rows, sb):
        rb = slice(r0, r0 + sb)
        cs = [_dot(cm, jnp.concatenate(_split_hi_lo(la_ref[rb, ks[h]]), axis=1)) for h in heads]
        cs = [c[:, :GLA_DK] + c[:, GLA_DK:] for c in cs]
        b = [c[:sb] for c in cs]
        b_tot = [c[sb:] for c in cs]
        v = [v_ref[rb, vs[h]] for h in heads]
        qg = [(q_ref[rb, ks[h]] * jnp.exp(b[h])).astype(BF16) for h in heads]
        kg = [(k_ref[rb, ks[h]] * jnp.exp(-b[h])).astype(BF16) for h in heads]
        kd = [(k_ref[rb, ks[h]] * jnp.exp(b_tot[h] - b[h])).astype(BF16) for h in heads]
        att = [jnp.where(causal, _dot_nt(qg[h], kg[h]), 0.0).astype(BF16) for h in heads]
        o_intra = [_dot(att[h], v[h]) for h in heads]
        s = [None] * GLA_HEADS
        for c in range(sb // CHUNK):
            seq = (r0 // CHUNK + c) // chunks_per_seq
            first_of_run = c == 0 or (r0 // CHUNK + c) % chunks_per_seq == 0
            last_of_run = c == sb // CHUNK - 1 or (r0 // CHUNK + c + 1) % chunks_per_seq == 0
            rs = slice(c * CHUNK, (c + 1) * CHUNK)
            ro = slice(r0 + c * CHUNK, r0 + (c + 1) * CHUNK)
            for h in heads:
                if first_of_run:
                    s[h] = s_scr[seq, h]
                o = o_intra[h][rs] + _dot(qg[h][rs], s[h].astype(BF16))
                o = o * lax.rsqrt(jnp.mean(o * o, axis=-1, keepdims=True) + LN_EPS) * gn[:, vs[h]]
                gate = gr_ref[ro, vs[h]]
                o_ref[ro, vs[h]] = (o * (gate * jax.nn.sigmoid(gate))).astype(BF16)
                decay = jnp.exp(jnp.broadcast_to(b_tot[h][c * CHUNK:c * CHUNK + 1], (GLA_DK, GLA_DK))).T
                s[h] = jnp.concatenate([decay, decay], axis=1) * s[h] + _dot_tn(kd[h][rs], v[h][rs])
                if last_of_run:
                    s_scr[seq, h] = s[h]

    @pl.when(t == pl.num_programs(1) - 1)
    def _():
        sout_ref[...] = s_scr[...]


def _gla(q, k, v, la, gr, gn, s0):
    bsz, t, _ = q.shape
    n_rows = min(GLA_ROWS, bsz * t)
    n_seq = max(1, n_rows // t)
    n_inner = max(1, t // n_rows)
    assert (bsz * t) % n_rows == 0 and bsz % n_seq == 0 and t % CHUNK == 0 and (t % n_rows == 0 or n_rows % t == 0)
    cm = _gla_cumsum_matrix(min(n_rows, GLA_BATCHED_ROWS))

    def rows(width):
        return pl.BlockSpec((n_rows, width), lambda b, i: (b * n_inner + i, 0))

    def flat(a):
        return a.reshape(bsz * t, a.shape[-1])

    s_spec = pl.BlockSpec((n_seq, GLA_HEADS, GLA_DK, GLA_DV), lambda b, i: (b, 0, 0, 0))
    o, s_new = pl.pallas_call(
        _gla_kernel,
        grid=(bsz // n_seq, n_inner),
        in_specs=[rows(GLA_QK), rows(GLA_QK), rows(GLA_V), rows(GLA_QK), rows(GLA_V), _const_spec(gn.shape),
                  s_spec, _const_spec(cm.shape)],
        out_specs=[rows(GLA_V), s_spec],
        out_shape=[jax.ShapeDtypeStruct((bsz * t, GLA_V), BF16),
                   jax.ShapeDtypeStruct((bsz, GLA_HEADS, GLA_DK, GLA_DV), F32)],
        scratch_shapes=[pltpu.VMEM((n_seq, GLA_HEADS, GLA_DK, GLA_DV), F32)],
        compiler_params=_params("parallel", "arbitrary"),
        name="gla",
    )(flat(q), flat(k), flat(v), flat(la), flat(gr), gn, s0, cm)
    return o, s_new


def _suffix_matrix(tk):
    r = lax.broadcasted_iota(jnp.int32, (2 * tk, tk), 0) % tk
    c = lax.broadcasted_iota(jnp.int32, (2 * tk, tk), 1)
    return (r >= c).astype(BF16)


def _sb_update(qs, ks, vs, carry_ref, acc_ref, suffix, visible, first, cmin_ref=None, stage_heads=None):
    sub = suffix.shape[1]
    n_sub = ks[0].shape[0] // sub
    cmin = None
    stage_heads = stage_heads or len(qs)
    for g0 in range(0, len(qs), stage_heads):
        heads = range(g0, min(g0 + stage_heads, len(qs)))
        z = {hd: _dot_nt(qs[hd], ks[hd]) for hd in heads}
        carry = {hd: None if first else carry_ref[hd] for hd in heads}
        ws = {hd: [None] * n_sub for hd in heads}
        for s in reversed(range(n_sub)):
            vis = None if visible is None else visible[s]
            zs = {hd: z[hd][:, s * sub:(s + 1) * sub] for hd in heads}
            if vis is not None:
                zs = {hd: jnp.where(vis, x, -jnp.inf) for hd, x in zs.items()}
            cs = {hd: jnp.maximum(x, 0.0) + jnp.log(1.0 + jnp.exp2(jnp.abs(x) * -LOG2E)) for hd, x in zs.items()}
            sums = {hd: _dot(jnp.concatenate(_split_hi_lo(c), axis=1), suffix) for hd, c in cs.items()}
            for hd in heads:
                e = zs[hd] - sums[hd]
                if carry[hd] is not None:
                    e = e - jnp.concatenate([carry[hd]] * (sub // LANES), axis=1)
                ws[hd][s] = jnp.exp(e).astype(BF16)
                total = jnp.broadcast_to(jnp.sum(cs[hd], axis=-1, keepdims=True), (cs[hd].shape[0], LANES))
                carry[hd] = total if carry[hd] is None else carry[hd] + total
        if cmin_ref is not None:
            group_min = jnp.min(functools.reduce(jnp.minimum, carry.values()))
            cmin = group_min if cmin is None else jnp.minimum(cmin, group_min)
        for hd in heads:
            pv = _dot(ws[hd][0] if n_sub == 1 else jnp.concatenate(ws[hd], axis=1), vs[hd])
            acc_ref[hd] = pv if first else acc_ref[hd] + pv
            carry_ref[hd] = carry[hd]
    if cmin_ref is not None:
        cmin_ref[0] = cmin


def _head(hd):
    return slice(hd * SB_DH, (hd + 1) * SB_DH)


def _sb_prompt_kernel(q_ref, kd_ref, vd_ref, kp_ref, vp_ref, k_hbm, v_hbm, sfx_ref, o_ref,
                      carry_scr, acc_scr, kbuf, vbuf, sem, cmin_scr):
    b, g, i = pl.program_id(0), pl.program_id(1), pl.program_id(2)
    tq = q_ref.shape[1]
    width = q_ref.shape[2]
    heads = width // SB_DH
    sfx = sfx_ref[...]
    rows = lax.broadcasted_iota(jnp.int32, (tq, tq), 0)
    cols = lax.broadcasted_iota(jnp.int32, (tq, tq), 1)
    strictly_earlier = cols < rows
    qs = [q_ref[0, :, _head(hd)] for hd in range(heads)]

    @pl.when(i == 0)
    def _():
        _sb_update(qs, [kd_ref[0, :, _head(hd)] for hd in range(heads)],
                   [vd_ref[0, :, _head(hd)] for hd in range(heads)],
                   carry_scr, acc_scr, sfx, (strictly_earlier,), True, cmin_scr, SB_STAGE_HEADS)

    @pl.when(i > 0)
    def _():
        _sb_update(qs, [jnp.concatenate([kp_ref[0, :, _head(hd)], kd_ref[0, :, _head(hd)]], axis=0)
                        for hd in range(heads)],
                   [jnp.concatenate([vp_ref[0, :, _head(hd)], vd_ref[0, :, _head(hd)]], axis=0)
                    for hd in range(heads)],
                   carry_scr, acc_scr, sfx, (None, strictly_earlier), True, cmin_scr, SB_STAGE_HEADS)

    n_older = jnp.maximum(i - 1, 0)

    def cond(state):
        j, cmin = state
        return (j < n_older) & (cmin <= SB_DEAD)

    def body(state):
        j, _ = state
        k0 = pl.multiple_of((i - 2 - j) * tq, tq)
        c0 = pl.multiple_of(g * width, LANES)
        copy_k = pltpu.make_async_copy(k_hbm.at[b, pl.ds(k0, tq), pl.ds(c0, width)], kbuf, sem.at[0])
        copy_v = pltpu.make_async_copy(v_hbm.at[b, pl.ds(k0, tq), pl.ds(c0, width)], vbuf, sem.at[1])
        copy_k.start()
        copy_v.start()
        copy_k.wait()
        copy_v.wait()
        _sb_update(qs, [kbuf[:, _head(hd)] for hd in range(heads)], [vbuf[:, _head(hd)] for hd in range(heads)],
                   carry_scr, acc_scr, sfx, None, False, cmin_scr, SB_STAGE_HEADS)
        return j + 1, cmin_scr[0]

    lax.while_loop(cond, body, (jnp.int32(0), cmin_scr[0]))
    for hd in range(heads):
        o_ref[0, :, _head(hd)] = acc_scr[hd].astype(o_ref.dtype)


def _sb_prompt(sq, sk, sv):
    bsz, t, _ = sq.shape
    blk = SB_QBLOCK
    assert t % blk == 0
    width = SB_GROUP * SB_DH
    sfx = _suffix_matrix(blk)
    own = pl.BlockSpec((1, blk, width), lambda b, g, i: (b, i, g))
    before = pl.BlockSpec((1, blk, width), lambda b, g, i: (b, jnp.maximum(i - 1, 0), g))
    hbm_spec = pl.BlockSpec(memory_space=pl.ANY)
    return pl.pallas_call(
        _sb_prompt_kernel,
        grid=(bsz, SB_HEADS // SB_GROUP, t // blk),
        in_specs=[own, own, own, before, before, hbm_spec, hbm_spec, _const_spec(sfx.shape)],
        out_specs=own,
        out_shape=jax.ShapeDtypeStruct((bsz, t, SB_W), BF16),
        scratch_shapes=[pltpu.VMEM((SB_GROUP, blk, LANES), F32), pltpu.VMEM((SB_GROUP, blk, SB_DH), F32),
                        pltpu.VMEM((blk, width), BF16), pltpu.VMEM((blk, width), BF16),
                        pltpu.SemaphoreType.DMA((2,)), pltpu.SMEM((1,), F32)],
        compiler_params=_params("parallel", "parallel", "arbitrary"),
        name="sb_prompt",
    )(sq, sk, sv, sk, sv, sk, sv, sfx)


def _sb_sample_kernel(q_ref, kn_ref, vn_ref, kw_ref, vw_ref, k_hbm, v_hbm, sfx_ref, o_ref,
                      carry_scr, acc_scr, kbuf, vbuf, sem, cmin_scr, *, n_past):
    b0 = pl.program_id(0) * q_ref.shape[0]
    n_seq, tq, _ = q_ref.shape
    blk = SB_BLOCK
    win = blk - tq
    sfx = sfx_ref[...]
    rows = lax.broadcasted_iota(jnp.int32, (tq, blk), 0)
    cols = lax.broadcasted_iota(jnp.int32, (tq, blk), 1)
    slots = [(si, hd) for si in range(n_seq) for hd in range(SB_HEADS)]

    def head_rows(ref, hd, first, n):
        return ref[pl.ds(first * SB_HEADS + hd, n, stride=SB_HEADS), :].astype(BF16)

    visible = (cols < win) | (cols - win < rows)
    qs = [q_ref[si, :, _head(hd)] for si, hd in slots]
    _sb_update(qs,
               [jnp.concatenate([head_rows(kw_ref.at[si], hd, blk - win, win), kn_ref[si, :, _head(hd)]], axis=0)
                for si, hd in slots],
               [jnp.concatenate([head_rows(vw_ref.at[si], hd, blk - win, win), vn_ref[si, :, _head(hd)]], axis=0)
                for si, hd in slots],
               carry_scr, acc_scr, sfx, (visible,), True, cmin_scr)

    n_left = n_past - win
    n_older = -(-n_left // blk)

    def cond(state):
        j, cmin = state
        return (j < n_older) & (cmin <= SB_DEAD)

    def body(state):
        j, _ = state
        end = n_left - j * blk
        start = jnp.maximum(end - blk, 0)
        row0 = pl.multiple_of(start * SB_HEADS, SB_HEADS)
        copies = [pltpu.make_async_copy(hbm.at[b0 + si, pl.ds(row0, blk * SB_HEADS)], buf.at[si], sem.at[w, si])
                  for si in range(n_seq) for w, (hbm, buf) in enumerate(((k_hbm, kbuf), (v_hbm, vbuf)))]
        for copy in copies:
            copy.start()
        for copy in copies:
            copy.wait()
        unread = cols < end - start
        _sb_update(qs, [head_rows(kbuf.at[si], hd, 0, blk) for si, hd in slots],
                   [head_rows(vbuf.at[si], hd, 0, blk) for si, hd in slots],
                   carry_scr, acc_scr, sfx, (unread,), False, cmin_scr)
        return j + 1, cmin_scr[0]

    lax.while_loop(cond, body, (jnp.int32(0), cmin_scr[0]))
    for slot, (si, hd) in enumerate(slots):
        o_ref[si, :, _head(hd)] = acc_scr[slot].astype(o_ref.dtype)


def _sb_sample(sq, sk, sv, k_past, v_past):
    bsz, t, _ = sq.shape
    p = k_past.shape[1]
    blk = SB_BLOCK
    n_seq = SB_SAMPLE_SEQS if bsz % SB_SAMPLE_SEQS == 0 else 1
    assert p % blk == 0 and p >= blk and t < blk and t % 16 == 0
    k_past = k_past.reshape(bsz, p * SB_HEADS, SB_DH)
    v_past = v_past.reshape(bsz, p * SB_HEADS, SB_DH)
    sfx = _suffix_matrix(blk)
    new_spec = pl.BlockSpec((n_seq, t, SB_W), lambda b: (b, 0, 0))
    window_spec = pl.BlockSpec((n_seq, blk * SB_HEADS, SB_DH), lambda b: (b, p // blk - 1, 0))
    hbm_spec = pl.BlockSpec(memory_space=pl.ANY)
    slots = n_seq * SB_HEADS
    return pl.pallas_call(
        functools.partial(_sb_sample_kernel, n_past=p),
        grid=(bsz // n_seq,),
        in_specs=[new_spec, new_spec, new_spec, window_spec, window_spec, hbm_spec, hbm_spec,
                  _const_spec(sfx.shape)],
        out_specs=new_spec,
        out_shape=jax.ShapeDtypeStruct((bsz, t, SB_W), BF16),
        scratch_shapes=[pltpu.VMEM((slots, t, LANES), F32), pltpu.VMEM((slots, t, SB_DH), F32),
                        pltpu.VMEM((n_seq, blk * SB_HEADS, SB_DH), F32),
                        pltpu.VMEM((n_seq, blk * SB_HEADS, SB_DH), F32),
                        pltpu.SemaphoreType.DMA((2, n_seq)), pltpu.SMEM((1,), F32)],
        compiler_params=_params("arbitrary"),
        name="sb_sample",
    )(sq, sk, sv, k_past, v_past, k_past, v_past, sfx)


def _merge_ffn_kernel(h_ref, oa_ref, ob_ref, ga_ref, gb_ref, wa_ref, wb_ref, wm_ref, g2_ref, b2_ref,
                      wg_ref, wu_ref, wo_ref, g3_ref, b3_ref, o_ref):
    branch_a = _dot(oa_ref[...], wa_ref[...])
    branch_b = _dot(ob_ref[...], wb_ref[...])
    merged = jax.nn.sigmoid(ga_ref[...]) * branch_a + jax.nn.sigmoid(gb_ref[...]) * branch_b
    mix = _dot(merged.astype(BF16), wm_ref[...])
    h2 = _layer_norm(DN_ALPHA * h_ref[...] + mix, g2_ref[...], b2_ref[...])
    o_ref[...] = _swiglu_ln(h2, wg_ref, wu_ref, wo_ref, g3_ref, b3_ref)


def _merge_ffn(groups, consts):
    outs = _rowwise_call(_merge_ffn_kernel, groups, consts, [(D_MODEL, F32)], TAIL_ROW_TILE, "merge_ffn")
    return [o[0] for o in outs]


def _cast_kernel(*refs, plans):
    n_in = len(plans)
    outs = iter(refs[n_in:])
    for ref, plan in zip(refs[:n_in], plans):
        x = ref[...]
        for lo, hi, keep in plan:
            y = x[:, lo:hi]
            if keep < hi - lo:
                y = jnp.where(lax.broadcasted_iota(jnp.int32, y.shape, 1) < keep, y, 0.0)
            next(outs)[...] = y.astype(BF16)


def _cast_weights(arrays, plans, rows_per_step):
    rows = arrays[0].shape[0]
    assert all(a.shape[0] == rows for a in arrays) and rows % rows_per_step == 0
    out_shape = [jax.ShapeDtypeStruct((rows, hi - lo), BF16) for plan in plans for lo, hi, _ in plan]
    return pl.pallas_call(
        functools.partial(_cast_kernel, plans=plans),
        grid=(rows // rows_per_step,),
        in_specs=[pl.BlockSpec((rows_per_step, a.shape[1]), lambda i: (i, 0)) for a in arrays],
        out_specs=[pl.BlockSpec((rows_per_step, o.shape[1]), lambda i: (i, 0)) for o in out_shape],
        out_shape=out_shape,
        compiler_params=_params("parallel"),
        name="cast_weights",
    )(*arrays)


def _row(v):
    return v.reshape(1, -1)


def _layer(xs, s0s, pasts, wts):
    shapes = [x.shape[:2] for x in xs]
    hs = _ffn_ln([x.reshape(-1, D_MODEL) for x in xs], wts["ffn1"])
    projected = _proj(hs, wts["proj"])
    tails, states, new_kv = [], [], []
    for (bsz, t), h, s0, past, outs in zip(shapes, hs, s0s, pasts, projected):
        gq, gk, gv, gr, la, sq, sk, sv, ga, gb, skb, svb = outs

        def seq(a):
            return a.reshape(bsz, t, a.shape[-1])

        o_a, s_new = _gla(seq(gq), seq(gk), seq(gv), seq(la), seq(gr), wts["gla_norm"], s0)
        if past is None:
            o_b = _sb_prompt(seq(sq), seq(skb), seq(svb))
        else:
            o_b = _sb_sample(seq(sq), seq(skb), seq(svb), *past)
        tails.append([h, o_a.reshape(-1, GLA_V), o_b.reshape(-1, SB_W), ga, gb])
        states.append(s_new)
        new_kv.append((sk.reshape(bsz, t, SB_HEADS, SB_DH), sv.reshape(bsz, t, SB_HEADS, SB_DH)))
    ys = _merge_ffn(tails, wts["merge"] + wts["ffn2"])
    return [(y.reshape(bsz, t, D_MODEL), s_new, k_new, v_new)
            for (bsz, t), y, s_new, (k_new, v_new) in zip(shapes, ys, states, new_kv)]


def kernel(x_prompt, x_sample, state_gla, cache_sb_k, cache_sb_v, ffn1_w_in, ffn1_w_out, ln1_g, ln1_b, w_in, w_gla_gate_up, b_gla_gate, g_gla_norm, w_gla_o, w_sb_o, w_out, ln2_g, ln2_b, ffn2_w_in, ffn2_w_out, ln3_g, ln3_b):
    xp, xs = x_prompt, x_sample
    bp = xp.shape[0]
    outs = [[] for _ in range(6)]
    lr_lo = 2 * GLA_QK + 2 * GLA_V
    lr_hi = lr_lo + GLA_GATE_RANK
    whole = ((0, D_MODEL, D_MODEL),)
    halves = ((0, D_FF, D_FF), (D_FF, 2 * D_FF, D_FF))
    for l in range(DEPTH):
        wg1, wu1, wg2, wu2, w_ga, w_sb, w_o = _cast_weights(
            [ffn1_w_in[l], ffn2_w_in[l], w_gla_o[l], w_sb_o[l], w_out[l]],
            (halves, halves, whole, whole, whole), CAST_ROWS)
        w_in_t = jnp.swapaxes(w_in[l], 0, 1)
        w_a = w_in_t[:lr_lo].astype(BF16)
        w_lr = jnp.pad(w_in_t[lr_lo:lr_hi], ((0, LANES - GLA_GATE_RANK), (0, 0))).astype(BF16)
        w_b = w_in_t[lr_hi:].astype(BF16)
        wo1, wo2 = _cast_weights([ffn1_w_out[l], ffn2_w_out[l]], (whole, whole), D_FF // 4)
        w_up = jnp.pad(w_gla_gate_up[l], ((0, LANES - GLA_GATE_RANK), (0, 0))).astype(BF16)
        wts = {
            "ffn1": (wg1, wu1, wo1, _row(ln1_g[l]), _row(ln1_b[l])),
            "proj": (w_a, w_b, w_lr, w_up, _row(b_gla_gate[l])),
            "gla_norm": _row(g_gla_norm[l]),
            "merge": (w_ga, w_sb, w_o, _row(ln2_g[l]), _row(ln2_b[l])),
            "ffn2": (wg2, wu2, wo2, _row(ln3_g[l]), _row(ln3_b[l])),
        }
        s0_p = jnp.zeros((bp, GLA_HEADS, GLA_DK, GLA_DV), F32)
        (xp, sp, kp, vp), (xs, ss, ksn, vsn) = _layer(
            [xp, xs], [s0_p, state_gla[l]], [None, (cache_sb_k[l], cache_sb_v[l])], wts)
        for lst, val in zip(outs, (sp, kp, vp, ss, ksn, vsn)):
            lst.append(val)
    return (xp, xs, *(jnp.stack(o) for o in outs))
```
